```python
import jax, jax.numpy as jnp
from jax import lax
import numpy as np

D_MODEL = 1024
BATCH = 4
SEQ = 8192
DEPTH = 1

D_RNN = 5 * D_MODEL // 4
RG_BLOCKS = 16
RG_BLOCK_W = D_RNN // RG_BLOCKS
RG_CONV = 4
RG_C = 8.0
DN_QK_HEADS = D_MODEL // 128
DN_V_HEADS = 2 * DN_QK_HEADS
DN_DK = 128
DN_DV = 128
DN_QK = DN_QK_HEADS * DN_DK
DN_V = DN_V_HEADS * DN_DV
DN_CONV = 4
DN_CHUNK = 64
D_FF = 11 * D_MODEL // 4
FFN_CONV = 3
LN_EPS = 1e-5
RMS_EPS = 1e-6
L2_EPS = 1e-6
DEEPNORM_ALPHA = (2 * DEPTH) ** 0.25
DEEPNORM_BETA = (8 * DEPTH) ** -0.25
IN_SPLITS = (D_RNN, D_RNN, DN_QK, DN_QK, DN_V, DN_V, DN_V_HEADS, DN_V_HEADS, D_MODEL, D_MODEL)
D_IN = sum(IN_SPLITS)

kernel_name = "hybrid_rglru_gdn_convffn_deepnorm_adaln"

F32 = jnp.float32


def split_cols(t, sizes):
    idx = np.cumsum(sizes)[:-1].tolist()
    return jnp.split(t, idx, axis=-1)


def causal_dwconv(x, w):
    k, ch = w.shape
    return lax.conv_general_dilated(
        x, w[:, None, :].astype(x.dtype), window_strides=(1,), padding=[(k - 1, 0)],
        dimension_numbers=('NWC', 'WIO', 'NWC'), feature_group_count=ch)


def layer_norm(x, g, b):
    xf = x.astype(F32)
    mu = jnp.mean(xf, axis=-1, keepdims=True)
    xc = xf - mu
    var = jnp.mean(xc * xc, axis=-1, keepdims=True)
    return (xc * lax.rsqrt(var + LN_EPS) * g.astype(F32) + b.astype(F32)).astype(x.dtype)


def l2_normalize(t):
    return t * lax.rsqrt(jnp.sum(t * t, axis=-1, keepdims=True) + L2_EPS)


def rg_lru(xr, w_a, b_a, w_x, b_x, lam):
    bsz, s, _ = xr.shape
    xf = xr.astype(F32)
    xb = xf.reshape(bsz, s, RG_BLOCKS, RG_BLOCK_W)
    gate_r = jax.nn.sigmoid(jnp.einsum('bsni,nij->bsnj', xb, w_a.astype(F32)).reshape(bsz, s, D_RNN) + b_a.astype(F32))
    gate_i = jax.nn.sigmoid(jnp.einsum('bsni,nij->bsnj', xb, w_x.astype(F32)).reshape(bsz, s, D_RNN) + b_x.astype(F32))
    log_a = -RG_C * gate_r * jax.nn.softplus(-lam.astype(F32))
    a = jnp.exp(log_a)
    mult = jnp.sqrt(-jnp.expm1(2.0 * log_a))
    u = mult * gate_i * xf

    def combine(left, right):
        a_l, h_l = left
        a_r, h_r = right
        return a_l * a_r, a_r * h_l + h_r

    _, h = lax.associative_scan(combine, (a, u), axis=1)
    return h


def chunk_gated_delta_rule(q, k, v, g, beta):
    bsz, nh, s, dk = q.shape
    dv = v.shape[-1]
    n = s // DN_CHUNK
    c = DN_CHUNK
    rs = lambda t: t.reshape((bsz, nh, n, c) + t.shape[3:])
    q, k, v, g, beta = rs(q), rs(k), rs(v), rs(g), rs(beta)
    G = jnp.cumsum(g, axis=-1)
    causal = jnp.tril(jnp.ones((c, c), dtype=bool))
    strict = jnp.tril(jnp.ones((c, c), dtype=bool), k=-1)
    diff = G[..., :, None] - G[..., None, :]
    decay = jnp.exp(jnp.where(causal, diff, -jnp.inf))
    kb = k * beta[..., None]
    A = jnp.where(strict, jnp.einsum('bhnid,bhnjd->bhnij', kb, k) * decay, 0.0)
    T = A + jnp.eye(c, dtype=F32)
    u = lax.linalg.triangular_solve(T, v * beta[..., None], left_side=True, lower=True)
    w = lax.linalg.triangular_solve(T, kb * jnp.exp(G)[..., None], left_side=True, lower=True)
    qk = jnp.einsum('bhnid,bhnjd->bhnij', q, k) * decay
    q_dec = q * jnp.exp(G)[..., None]
    k_dec = k * jnp.exp(G[..., -1:] - G)[..., None]
    g_last = jnp.exp(G[..., -1])
    xs = tuple(jnp.moveaxis(t, 2, 0) for t in (qk, q_dec, k_dec, u, w, g_last))

    def step(state, inp):
        qk_n, qd_n, kd_n, u_n, w_n, gl_n = inp
        v_new = u_n - jnp.einsum('bhcd,bhde->bhce', w_n, state)
        o = jnp.einsum('bhcd,bhde->bhce', qd_n, state) + jnp.einsum('bhij,bhje->bhie', qk_n, v_new)
        state = gl_n[..., None, None] * state + jnp.einsum('bhcd,bhce->bhde', kd_n, v_new)
        return state, o

    _, o = lax.scan(step, jnp.zeros((bsz, nh, dk, dv), F32), xs)
    return jnp.moveaxis(o, 0, 2).reshape(bsz, nh, s, dv)


def gated_deltanet(q, k, v, z, a_in, b_in, conv_w, a_log, dt_bias, norm_w):
    bsz, s, _ = q.shape
    qkv = jax.nn.silu(causal_dwconv(jnp.concatenate([q, k, v], axis=-1).astype(F32), conv_w.astype(F32)))
    q, k, v = split_cols(qkv, (DN_QK, DN_QK, DN_V))
    rep = DN_V_HEADS // DN_QK_HEADS
    q = jnp.repeat(l2_normalize(q.reshape(bsz, s, DN_QK_HEADS, DN_DK)), rep, axis=2) * (DN_DK ** -0.5)
    k = jnp.repeat(l2_normalize(k.reshape(bsz, s, DN_QK_HEADS, DN_DK)), rep, axis=2)
    v = v.reshape(bsz, s, DN_V_HEADS, DN_DV)
    beta = jax.nn.sigmoid(b_in.astype(F32))
    g = -jnp.exp(a_log.astype(F32)) * jax.nn.softplus(a_in.astype(F32) + dt_bias.astype(F32))
    o = chunk_gated_delta_rule(jnp.swapaxes(q, 1, 2), jnp.swapaxes(k, 1, 2), jnp.swapaxes(v, 1, 2),
                               jnp.swapaxes(g, 1, 2), jnp.swapaxes(beta, 1, 2))
    o = jnp.swapaxes(o, 1, 2)
    o = o * lax.rsqrt(jnp.mean(o * o, axis=-1, keepdims=True) + RMS_EPS) * norm_w.astype(F32)
    o = o * jax.nn.silu(z.astype(F32).reshape(bsz, s, DN_V_HEADS, DN_DV))
    return o.reshape(bsz, s, DN_V)


def token_mixer(h, w_in, rg_conv_w, rg_conv_b, rg_w_a, rg_b_a, rg_w_x, rg_b_x, rg_lambda,
                dn_conv_w, dn_a_log, dn_dt_bias, dn_norm_w, w_proj_a, w_proj_b, w_out):
    proj = h @ w_in
    xr, gr, q, k, v, z, a_in, b_in, g_a, g_b = split_cols(proj, IN_SPLITS)
    xr = causal_dwconv(xr, rg_conv_w) + rg_conv_b
    rec = rg_lru(xr, rg_w_a, rg_b_a, rg_w_x, rg_b_x, rg_lambda) * jax.nn.gelu(gr.astype(F32))
    y_a = rec.astype(h.dtype) @ w_proj_a
    dn = gated_deltanet(q, k, v, z, a_in, b_in, dn_conv_w, dn_a_log, dn_dt_bias, dn_norm_w)
    y_b = dn.astype(h.dtype) @ w_proj_b
    merged = jax.nn.sigmoid(g_a) * y_a + jax.nn.sigmoid(g_b) * y_b
    return merged @ w_out


def conv_ffn(h, w_gate, w_up, conv_w, conv_b, w_down):
    gate = causal_dwconv(h @ w_gate, conv_w) + conv_b
    return (jax.nn.gelu(gate) * (h @ w_up)) @ w_down


def setup_inputs(seed: int = 0) -> dict:
    key = jax.random.key(seed)
    ks = jax.random.split(key, 32)
    L, D = DEPTH, D_MODEL
    nrm = lambda kk, shape, scale: jax.random.normal(kk, shape, F32) * scale
    u_a = jax.random.uniform(ks[9], (L, D_RNN), F32, 0.9, 0.999)
    s_a = u_a ** (1.0 / RG_C)
    rg_lambda = jnp.log(s_a) - jnp.log1p(-s_a)
    dt = jnp.exp(jax.random.uniform(ks[12], (L, DN_V_HEADS), F32, np.log(1e-3), np.log(1e-1)))
    dt = jnp.maximum(dt, 1e-4)
    return {
        "x": nrm(ks[0], (BATCH, SEQ, D), 1.0),
        "c": nrm(ks[1], (BATCH, D), 1.0),
        "w_ada": nrm(ks[2], (L, D, 6 * D), 0.1 * D ** -0.5),
        "b_ada": nrm(ks[3], (L, 6 * D), 0.01),
        "w_in": nrm(ks[4], (L, D, D_IN), D ** -0.5),
        "rg_conv_w": nrm(ks[5], (L, RG_CONV, D_RNN), RG_CONV ** -0.5),
        "rg_conv_b": nrm(ks[6], (L, D_RNN), 0.01),
        "rg_w_a": nrm(ks[7], (L, RG_BLOCKS, RG_BLOCK_W, RG_BLOCK_W), RG_BLOCK_W ** -0.5),
        "rg_b_a": nrm(ks[8], (L, D_RNN), 0.01),
        "rg_w_x": nrm(ks[10], (L, RG_BLOCKS, RG_BLOCK_W, RG_BLOCK_W), RG_BLOCK_W ** -0.5),
        "rg_b_x": nrm(ks[11], (L, D_RNN), 0.01),
        "rg_lambda": rg_lambda,
        "dn_conv_w": nrm(ks[13], (L, DN_CONV, 2 * DN_QK + DN_V), DN_CONV ** -0.5),
        "dn_a_log": jnp.log(jax.random.uniform(ks[14], (L, DN_V_HEADS), F32, 1.0, 16.0)),
        "dn_dt_bias": dt + jnp.log(-jnp.expm1(-dt)),
        "dn_norm_w": 1.0 + nrm(ks[15], (L, DN_DV), 0.02),
        "w_proj_a": nrm(ks[16], (L, D_RNN, D), D_RNN ** -0.5),
        "w_proj_b": nrm(ks[17], (L, DN_V, D), DN_V ** -0.5),
        "w_out": nrm(ks[18], (L, D, D), DEEPNORM_BETA * D ** -0.5),
        "ln1_g": 1.0 + nrm(ks[19], (L, D), 0.02),
        "ln1_b": nrm(ks[20], (L, D), 0.01),
        "ffn_w_gate": nrm(ks[21], (L, D, D_FF), D ** -0.5),
        "ffn_w_up": nrm(ks[22], (L, D, D_FF), D ** -0.5),
        "ffn_conv_w": nrm(ks[23], (L, FFN_CONV, D_FF), FFN_CONV ** -0.5),
        "ffn_conv_b": nrm(ks[24], (L, D_FF), 0.01),
        "ffn_w_down": nrm(ks[25], (L, D_FF, D), DEEPNORM_BETA * D_FF ** -0.5),
        "ln2_g": 1.0 + nrm(ks[26], (L, D), 0.02),
        "ln2_b": nrm(ks[27], (L, D), 0.01),
    }


def reference(x, c, w_ada, b_ada, w_in, rg_conv_w, rg_conv_b, rg_w_a, rg_b_a, rg_w_x, rg_b_x,
              rg_lambda, dn_conv_w, dn_a_log, dn_dt_bias, dn_norm_w, w_proj_a, w_proj_b, w_out,
              ln1_g, ln1_b, ffn_w_gate, ffn_w_up, ffn_conv_w, ffn_conv_b, ffn_w_down, ln2_g, ln2_b):
    for l in range(DEPTH):
        ada = jax.nn.silu(c) @ w_ada[l] + b_ada[l]
        sh1, sc1, gt1, sh2, sc2, gt2 = [t[:, None, :] for t in jnp.split(ada, 6, axis=-1)]
        h = x * (1.0 + sc1) + sh1
        mix = token_mixer(h, w_in[l], rg_conv_w[l], rg_conv_b[l], rg_w_a[l], rg_b_a[l], rg_w_x[l],
                          rg_b_x[l], rg_lambda[l], dn_conv_w[l], dn_a_log[l], dn_dt_bias[l],
                          dn_norm_w[l], w_proj_a[l], w_proj_b[l], w_out[l])
        x = layer_norm(DEEPNORM_ALPHA * x + (1.0 + gt1) * mix, ln1_g[l], ln1_b[l])
        h = x * (1.0 + sc2) + sh2
        ff = conv_ffn(h, ffn_w_gate[l], ffn_w_up[l], ffn_conv_w[l], ffn_conv_b[l], ffn_w_down[l])
        x = layer_norm(DEEPNORM_ALPHA * x + (1.0 + gt2) * ff, ln2_g[l], ln2_b[l])
    return x
```

```python
import functools

import jax
import jax.numpy as jnp
from jax import lax
from jax.experimental import pallas as pl
from jax.experimental.pallas import tpu as pltpu

F32 = jnp.float32
BF16 = jnp.bfloat16

LANES = 128
SUBLANES = 8
VMEM_LIMIT_BYTES = 56 * 1024 * 1024

RG_BLOCKS = 16
RG_CONV = 4
RG_C = 8.0
DN_DK = 128
DN_DV = 128
DN_CONV = 4
CHUNK = 64
FFN_CONV = 3
LN_EPS = 1e-5
RMS_EPS = 1e-6
L2_EPS = 1e-6
HALO = SUBLANES
AB_PAD = LANES


def _cparams(sem):
    return pltpu.CompilerParams(dimension_semantics=sem, vmem_limit_bytes=VMEM_LIMIT_BYTES)


def _sigmoid(x):
    return 1.0 / (1.0 + jnp.exp(-x))


def _silu(x):
    return x * _sigmoid(x)


def _softplus(x):
    return jnp.maximum(x, 0.0) + jnp.log(1.0 + jnp.exp(-jnp.abs(x)))


def _gelu_tanh(x):
    return 0.5 * x * (1.0 + jnp.tanh(0.7978845608028654 * (x + 0.044715 * (x * x * x))))


def _layer_norm(x, g, b):
    mu = jnp.mean(x, axis=-1, keepdims=True)
    xc = x - mu
    var = jnp.mean(xc * xc, axis=-1, keepdims=True)
    return xc * lax.rsqrt(var + LN_EPS) * g + b


def _causal_conv(buf_ref, w, n_rows, width):
    acc = None
    for k in range(width):
        off = HALO - (width - 1) + k
        term = w[k:k + 1, :] * buf_ref[off:off + n_rows, :]
        acc = term if acc is None else acc + term
    return acc


def _adaln_kernel(c_ref, w_ref, b_ref, o_ref):
    c = c_ref[...]
    o_ref[...] = jnp.dot(_silu(c), w_ref[...], preferred_element_type=F32,
                         precision=lax.Precision.HIGHEST) + b_ref[...]


def _adaln(c, w_ada, b_ada):
    bsz, d = c.shape
    n = w_ada.shape[1]
    rows = -(-bsz // SUBLANES) * SUBLANES
    c_pad = jnp.zeros((rows, d), F32).at[:bsz].set(c)
    tn = 1536
    out = pl.pallas_call(
        _adaln_kernel,
        out_shape=jax.ShapeDtypeStruct((rows, n), F32),
        grid=(n // tn,),
        in_specs=[pl.BlockSpec((rows, d), lambda j: (0, 0)),
                  pl.BlockSpec((d, tn), lambda j: (0, j)),
                  pl.BlockSpec((1, tn), lambda j: (0, j))],
        out_specs=pl.BlockSpec((rows, tn), lambda j: (0, j)),
        compiler_params=_cparams(("arbitrary",)),
        name="adaln",
    )(c_pad, w_ada, b_ada.reshape(1, n))
    return out[:bsz]


def _inproj_kernel(x_ref, sc_ref, sh_ref, w_ref, o_ref, h_scr):
    @pl.when(pl.program_id(1) == 0)
    def _():
        h = x_ref[...] * (1.0 + sc_ref[0]) + sh_ref[0]
        h_scr[...] = h.astype(BF16)

    o_ref[...] = jnp.dot(h_scr[...], w_ref[...], preferred_element_type=F32)


def _inproj(x2, sc, sh, w, seq, tm, tn):
    m, d = x2.shape
    n = w.shape[1]
    return pl.pallas_call(
        _inproj_kernel,
        out_shape=jax.ShapeDtypeStruct((m, n), F32),
        grid=(m // tm, n // tn),
        in_specs=[pl.BlockSpec((tm, d), lambda i, j: (i, 0)),
                  pl.BlockSpec((1, 1, d), lambda i, j: ((i * tm) // seq, 0, 0)),
                  pl.BlockSpec((1, 1, d), lambda i, j: ((i * tm) // seq, 0, 0)),
                  pl.BlockSpec((d, tn), lambda i, j: (0, j))],
        out_specs=pl.BlockSpec((tm, tn), lambda i, j: (i, j)),
        scratch_shapes=[pltpu.VMEM((tm, d), BF16)],
        compiler_params=_cparams(("arbitrary", "arbitrary")),
        name="inproj",
    )(x2, sc, sh, w)


RG_TILE_N = 256
RG_TILE_K = 512


def _rg_windows(d_rnn):
    bw = d_rnn // RG_BLOCKS
    starts = []
    for j in range(d_rnn // RG_TILE_N):
        lo = (j * RG_TILE_N) // bw * bw
        hi = -(-((j + 1) * RG_TILE_N) // bw) * bw
        k0 = min(lo // LANES * LANES, d_rnn - RG_TILE_K)
        assert k0 <= lo and hi <= k0 + RG_TILE_K
        starts.append(k0)
    return tuple(starts)


def _rglru_kernel(k_starts, xr_ref, gr_ref, cw_ref, cb_ref, wg_ref, ba_ref, bx_ref, lam_ref, wp_ref,
                  o_ref, xbuf, a_scr, u_scr, h_scr):
    ts, d_rnn = xr_ref.shape
    s = pl.program_id(1)

    @pl.when(s == 0)
    def _():
        xbuf[0:HALO, :] = jnp.zeros((HALO, d_rnn), F32)
        h_scr[...] = jnp.zeros_like(h_scr)

    xbuf[HALO:HALO + ts, :] = xr_ref[...]
    xc = _causal_conv(xbuf, cw_ref[...], ts, RG_CONV) + cb_ref[...]
    xbuf[0:HALO, :] = xbuf[ts:ts + HALO, :]

    xcb = xc.astype(BF16)
    pre_r, pre_i = [], []
    for j, k0 in enumerate(k_starts):
        r = jnp.dot(xcb[:, k0:k0 + RG_TILE_K], wg_ref[j], preferred_element_type=F32)
        pre_r.append(r[:, :RG_TILE_N])
        pre_i.append(r[:, RG_TILE_N:])
    gate_r = _sigmoid(jnp.concatenate(pre_r, axis=1) + ba_ref[...])
    gate_i = _sigmoid(jnp.concatenate(pre_i, axis=1) + bx_ref[...])
    log_a = (-RG_C) * gate_r * _softplus(-lam_ref[...])
    a = jnp.exp(log_a)
    a_scr[...] = a
    u_scr[...] = jnp.sqrt(1.0 - jnp.exp(2.0 * log_a)) * gate_i * xc

    row = lax.broadcasted_iota(jnp.int32, (SUBLANES, d_rnn), 0)

    def group(g, h_prev):
        r0 = pl.multiple_of(g * SUBLANES, SUBLANES)
        ag = a_scr[pl.ds(r0, SUBLANES), :]
        ug = u_scr[pl.ds(r0, SUBLANES), :]
        for d in (1, 2, 4):
            keep = row >= d
            a_sh = jnp.where(keep, pltpu.roll(ag, d, axis=0), 1.0)
            u_sh = jnp.where(keep, pltpu.roll(ug, d, axis=0), 0.0)
            ug = ug + ag * u_sh
            ag = ag * a_sh
        hg = ug + ag * h_prev
        u_scr[pl.ds(r0, SUBLANES), :] = hg
        return jnp.broadcast_to(hg[SUBLANES - 1:SUBLANES, :], (SUBLANES, d_rnn))

    h_scr[...] = lax.fori_loop(0, ts // SUBLANES, group, h_scr[...], unroll=4)

    rec = u_scr[...] * _gelu_tanh(gr_ref[...])
    o_ref[...] = jnp.dot(rec.astype(BF16), wp_ref[...], preferred_element_type=F32)


def _rglru(p, cw, cb, wg, ba, bx, lam, wp, bsz, seq, ts, k_starts):
    d_rnn = cw.shape[1]
    d = wp.shape[1]
    nst = seq // ts
    row_map = lambda b, s: (b * nst + s, 0)
    const2 = lambda b, s: (0, 0)
    return pl.pallas_call(
        functools.partial(_rglru_kernel, k_starts),
        out_shape=jax.ShapeDtypeStruct((bsz * seq, d), F32),
        grid=(bsz, nst),
        in_specs=[pl.BlockSpec((ts, d_rnn), lambda b, s: (b * nst + s, 0)),
                  pl.BlockSpec((ts, d_rnn), lambda b, s: (b * nst + s, 1)),
                  pl.BlockSpec(cw.shape, const2),
                  pl.BlockSpec((1, d_rnn), const2),
                  pl.BlockSpec(wg.shape, lambda b, s: (0, 0, 0)),
                  pl.BlockSpec((1, d_rnn), const2),
                  pl.BlockSpec((1, d_rnn), const2),
                  pl.BlockSpec((1, d_rnn), const2),
                  pl.BlockSpec(wp.shape, const2)],
        out_specs=pl.BlockSpec((ts, d), row_map),
        scratch_shapes=[pltpu.VMEM((ts + HALO, d_rnn), F32),
                        pltpu.VMEM((ts, d_rnn), F32),
                        pltpu.VMEM((ts, d_rnn), F32),
                        pltpu.VMEM((SUBLANES, d_rnn), F32)],
        compiler_params=_cparams(("arbitrary", "arbitrary")),
        name="rglru",
    )(p, p, cw, cb, wg, ba, bx, lam, wp)


def _chunk_cumsum(x):
    rows = x.shape[0]
    row = lax.broadcasted_iota(jnp.int32, x.shape, 0) % CHUNK
    d = 1
    while d < CHUNK:
        x = x + jnp.where(row >= d, pltpu.roll(x, d, axis=0), 0.0)
        d *= 2
    del rows
    return x


def _pair_blockdiag(m2, left):
    return jnp.concatenate([jnp.where(left, m2, 0.0), jnp.where(left, 0.0, m2)], axis=0)


def _gdn_kernel(q_ref, k_ref, v_ref, z_ref, ab_ref, cwq_ref, cwk_ref, cwv_ref, alog_ref, dtb_ref, nw_ref, wb_ref,
                o_ref,
                s_scr, hq_scr, hk_scr, hv_scr, cq, ck, cv,
                u_scr, w_scr, qd_scr, qk_scr, kdt_scr, gl_scr, dn_scr):
    bsz, tt, _ = q_ref.shape
    n_chunks = tt // CHUNK
    s = pl.program_id(0)
    hg = pl.program_id(1)

    @pl.when(s == 0)
    def _():
        hq_scr[hg] = jnp.zeros(hq_scr.shape[1:], F32)
        hk_scr[hg] = jnp.zeros(hk_scr.shape[1:], F32)
        hv_scr[hg] = jnp.zeros(hv_scr.shape[1:], F32)
        s_scr[2 * hg] = jnp.zeros(s_scr.shape[1:], F32)
        s_scr[2 * hg + 1] = jnp.zeros(s_scr.shape[1:], F32)

    lane = lax.broadcasted_iota(jnp.int32, (CHUNK, LANES), 1)
    row = lax.broadcasted_iota(jnp.int32, (CHUNK, LANES), 0)
    left = lane < CHUNK
    jj = lane % CHUNK
    causal = row >= jj
    strict = row > jj
    eye2 = jnp.where(row == jj, 1.0, 0.0)
    neg_a = -jnp.exp(alog_ref[...])
    dtb = dtb_ref[...]

    def lane_col(arr, col):
        r = pltpu.roll(arr, (LANES - col) % LANES, axis=1)
        return jnp.broadcast_to(r[:, 0:1], arr.shape)

    def prep(b, carry):
        def conv(x_ref, halo_scr, buf, cw_ref):
            buf[0:HALO, :] = halo_scr[hg, b]
            buf[HALO:HALO + tt, :] = x_ref[b]
            halo_scr[hg, b] = buf[tt:tt + HALO, :]
            return _silu(_causal_conv(buf, cw_ref[...], tt, DN_CONV))

        qc = conv(q_ref, hq_scr, cq, cwq_ref)
        kc = conv(k_ref, hk_scr, ck, cwk_ref)
        vc = conv(v_ref, hv_scr, cv, cwv_ref)
        qn = qc * (lax.rsqrt(jnp.sum(qc * qc, axis=-1, keepdims=True) + L2_EPS) * (DN_DK ** -0.5))
        kn = kc * lax.rsqrt(jnp.sum(kc * kc, axis=-1, keepdims=True) + L2_EPS)

        ab = ab_ref[b]
        g_all = neg_a * _softplus(ab + dtb)
        beta_all = _sigmoid(ab)
        n_heads = s_scr.shape[0]
        gb, bb = [], []
        for j in range(2):
            gb.append(_chunk_cumsum(lane_col(g_all, 2 * hg + j)))
            bb.append(lane_col(beta_all, n_heads + 2 * hg + j))

        for c in range(n_chunks):
            r0 = c * CHUNK
            rs = slice(r0, r0 + CHUNK)
            qn_c, kn_c = qn[rs], kn[rs]
            g0, g1 = gb[0][rs], gb[1][rs]
            b0, b1 = bb[0][rs], bb[1][rs]
            g_pair = jnp.where(left, g0, g1)
            b_pair = jnp.where(left, b0, b1)
            g_rows = jnp.transpose(jnp.concatenate([g0, g1], axis=0))[0:CHUNK]
            decay = jnp.where(causal, jnp.exp(jnp.where(causal, g_pair - g_rows, 0.0)), 0.0)

            kn_b = kn_c.astype(BF16)
            qk_lhs = jnp.concatenate([qn_c.astype(BF16), kn_b], axis=0)
            kk_rhs = jnp.concatenate([kn_b, kn_b], axis=0)
            kq = lax.dot_general(qk_lhs, kk_rhs, (((1,), (1,)), ((), ())), preferred_element_type=F32)
            qk2 = kq[0:CHUNK] * decay
            a2 = jnp.where(strict, b_pair * kq[CHUNK:2 * CHUNK] * decay, 0.0)

            m = -a2
            qsum = eye2 + m
            m = jnp.dot(m.astype(BF16), _pair_blockdiag(m, left).astype(BF16), preferred_element_type=F32)
            n_steps = CHUNK.bit_length() - 2
            for it in range(n_steps):
                bd = _pair_blockdiag(m, left).astype(BF16)
                if it < n_steps - 1:
                    r = jnp.dot(jnp.concatenate([qsum, m], axis=0).astype(BF16), bd, preferred_element_type=F32)
                    qsum = qsum + r[0:CHUNK]
                    m = r[CHUNK:2 * CHUNK]
                else:
                    qsum = qsum + jnp.dot(qsum.astype(BF16), bd, preferred_element_type=F32)

            eg0, eg1 = jnp.exp(g0), jnp.exp(g1)
            v0, v1 = vc[rs, 0:DN_DV], vc[rs, DN_DV:2 * DN_DV]
            rhs = jnp.concatenate([jnp.concatenate([b0 * v0, (b0 * eg0) * kn_c], axis=1),
                                   jnp.concatenate([b1 * v1, (b1 * eg1) * kn_c], axis=1)], axis=0)
            uw = jnp.dot(_pair_blockdiag(qsum, left).astype(BF16), rhs.astype(BF16), preferred_element_type=F32)

            gl0 = jnp.broadcast_to(g0[CHUNK - 1:CHUNK], g0.shape)
            gl1 = jnp.broadcast_to(g1[CHUNK - 1:CHUNK], g1.shape)
            kd = jnp.concatenate([kn_c * jnp.exp(gl0 - g0), kn_c * jnp.exp(gl1 - g1)], axis=0)
            kdt_scr[b, c] = jnp.transpose(kd).astype(BF16)
            qk_scr[b, c] = qk2.astype(BF16)
            for j, (eg, gl) in enumerate(((eg0, gl0), (eg1, gl1))):
                u_scr[j, b, c] = uw[j * CHUNK:(j + 1) * CHUNK, 0:DN_DV]
                w_scr[j, b, c] = uw[j * CHUNK:(j + 1) * CHUNK, DN_DV:2 * DN_DV].astype(BF16)
                qd_scr[j, b, c] = (qn_c * eg).astype(BF16)
                gl_scr[j, b, c] = jnp.exp(gl[0:SUBLANES])
        return carry

    lax.fori_loop(0, bsz, prep, 0)

    nw = nw_ref[...]
    zeros_v = jnp.zeros((CHUNK, DN_DV), F32)
    for c in range(n_chunks):
        for b in range(bsz):
            vnew, qs, st = [], [], []
            for j in range(2):
                s_h = s_scr[2 * hg + j, b]
                lhs = jnp.concatenate([w_scr[j, b, c], qd_scr[j, b, c]], axis=0)
                r1 = jnp.dot(lhs, s_h.astype(BF16), preferred_element_type=F32)
                vnew.append(u_scr[j, b, c] - r1[0:CHUNK])
                qs.append(r1[CHUNK:2 * CHUNK])
                st.append(s_h)
            bdv = jnp.concatenate([jnp.concatenate([vnew[0], zeros_v], axis=1),
                                   jnp.concatenate([zeros_v, vnew[1]], axis=1)], axis=0).astype(BF16)
            lhs2 = jnp.concatenate([qk_scr[b, c], kdt_scr[b, c]], axis=0)
            r2 = jnp.dot(lhs2, bdv, preferred_element_type=F32)
            outs = []
            for j in range(2):
                cols = slice(j * DN_DV, (j + 1) * DN_DV)
                o = qs[j] + r2[0:CHUNK, cols]
                s_scr[2 * hg + j, b] = gl_scr[j, b, c][0:1, :] * st[j] + r2[CHUNK:CHUNK + DN_DK, cols]
                o = o * lax.rsqrt(jnp.mean(o * o, axis=-1, keepdims=True) + RMS_EPS) * nw
                outs.append(o)
            zc = z_ref[b, c * CHUNK:(c + 1) * CHUNK, :]
            dn_scr[b, c * CHUNK:(c + 1) * CHUNK, :] = (jnp.concatenate(outs, axis=1) * _silu(zc)).astype(BF16)

    for b in range(bsz):
        y = jnp.dot(dn_scr[b], wb_ref[...], preferred_element_type=F32)

        @pl.when(hg == 0)
        def _():
            o_ref[b] = y

        @pl.when(hg != 0)
        def _():
            o_ref[b] += y


def _gdn(p3, cw, alog, dtb, nw, wb, col_q, col_k, col_v, col_z, col_ab, n_heads, tt):
    bsz, seq, _ = p3.shape
    d = wb.shape[1]
    n_groups = n_heads // 2
    nst = seq // tt
    nch = tt // CHUNK
    qk_w = n_groups * DN_DK
    assert col_q % DN_DK == 0 and col_k % DN_DK == 0 and col_v % (2 * DN_DV) == 0 and col_z % (2 * DN_DV) == 0
    in_specs = [
        pl.BlockSpec((bsz, tt, DN_DK), lambda s, g: (0, s, col_q // DN_DK + g)),
        pl.BlockSpec((bsz, tt, DN_DK), lambda s, g: (0, s, col_k // DN_DK + g)),
        pl.BlockSpec((bsz, tt, 2 * DN_DV), lambda s, g: (0, s, col_v // (2 * DN_DV) + g)),
        pl.BlockSpec((bsz, tt, 2 * DN_DV), lambda s, g: (0, s, col_z // (2 * DN_DV) + g)),
        pl.BlockSpec((bsz, tt, AB_PAD), lambda s, g: (0, s, col_ab // AB_PAD)),
        pl.BlockSpec((DN_CONV, DN_DK), lambda s, g: (0, g)),
        pl.BlockSpec((DN_CONV, DN_DK), lambda s, g: (0, qk_w // DN_DK + g)),
        pl.BlockSpec((DN_CONV, 2 * DN_DV), lambda s, g: (0, 2 * qk_w // (2 * DN_DV) + g)),
        pl.BlockSpec((1, LANES), lambda s, g: (0, 0)),
        pl.BlockSpec((1, LANES), lambda s, g: (0, 0)),
        pl.BlockSpec((1, DN_DV), lambda s, g: (0, 0)),
        pl.BlockSpec((2 * DN_DV, d), lambda s, g: (g, 0)),
    ]
    scratch = [
        pltpu.VMEM((n_heads, bsz, DN_DK, DN_DV), F32),
        pltpu.VMEM((n_groups, bsz, HALO, DN_DK), F32),
        pltpu.VMEM((n_groups, bsz, HALO, DN_DK), F32),
        pltpu.VMEM((n_groups, bsz, HALO, 2 * DN_DV), F32),
        pltpu.VMEM((tt + HALO, DN_DK), F32),
        pltpu.VMEM((tt + HALO, DN_DK), F32),
        pltpu.VMEM((tt + HALO, 2 * DN_DV), F32),
        pltpu.VMEM((2, bsz, nch, CHUNK, DN_DV), F32),
        pltpu.VMEM((2, bsz, nch, CHUNK, DN_DK), BF16),
        pltpu.VMEM((2, bsz, nch, CHUNK, DN_DK), BF16),
        pltpu.VMEM((bsz, nch, CHUNK, 2 * CHUNK), BF16),
        pltpu.VMEM((bsz, nch, DN_DK, 2 * CHUNK), BF16),
        pltpu.VMEM((2, bsz, nch, SUBLANES, DN_DV), F32),
        pltpu.VMEM((bsz, tt, 2 * DN_DV), BF16),
    ]
    return pl.pallas_call(
        _gdn_kernel,
        out_shape=jax.ShapeDtypeStruct((bsz, seq, d), F32),
        grid=(nst, n_groups),
        in_specs=in_specs,
        out_specs=pl.BlockSpec((bsz, tt, d), lambda s, g: (0, s, 0)),
        scratch_shapes=scratch,
        compiler_params=_cparams(("arbitrary", "arbitrary")),
        name="gdn",
    )(p3, p3, p3, p3, p3, cw, cw, cw, alog, dtb, nw, wb)


def _merge_kernel(alpha, x_ref, ya_ref, yb_ref, ga0_ref, ga1_ref, gb0_ref, gb1_ref, gt_ref, wo_ref, lg_ref, lb_ref,
                  o_ref):
    ga = jnp.concatenate([ga0_ref[...], ga1_ref[...]], axis=1)
    gb = jnp.concatenate([gb0_ref[...], gb1_ref[...]], axis=1)
    merged = _sigmoid(ga) * ya_ref[...] + _sigmoid(gb) * yb_ref[...]
    mix = jnp.dot(merged.astype(BF16), wo_ref[...], preferred_element_type=F32)
    pre = alpha * x_ref[...] + (1.0 + gt_ref[0]) * mix
    o_ref[...] = _layer_norm(pre, lg_ref[...], lb_ref[...])


def _merge(x2, ya, yb, p, gt, wo, lg, lb, col_ga, col_gb, seq, tm, alpha):
    m, d = x2.shape
    half = d // 2
    assert col_ga % half == 0 and col_gb % half == 0
    row = lambda i: (i, 0)
    const = lambda i: (0, 0)
    return pl.pallas_call(
        functools.partial(_merge_kernel, alpha),
        out_shape=jax.ShapeDtypeStruct((m, d), F32),
        grid=(m // tm,),
        in_specs=[pl.BlockSpec((tm, d), row), pl.BlockSpec((tm, d), row), pl.BlockSpec((tm, d), row),
                  pl.BlockSpec((tm, half), lambda i: (i, col_ga // half)),
                  pl.BlockSpec((tm, half), lambda i: (i, col_ga // half + 1)),
                  pl.BlockSpec((tm, half), lambda i: (i, col_gb // half)),
                  pl.BlockSpec((tm, half), lambda i: (i, col_gb // half + 1)),
                  pl.BlockSpec((1, 1, d), lambda i: ((i * tm) // seq, 0, 0)),
                  pl.BlockSpec((d, d), const), pl.BlockSpec((1, d), const), pl.BlockSpec((1, d), const)],
        out_specs=pl.BlockSpec((tm, d), row),
        compiler_params=_cparams(("arbitrary",)),
        name="merge",
    )(x2, ya, yb, p, p, p, p, gt, wo, lg, lb)


def _ffn_kernel(alpha, n_split, x_ref, sc_ref, sh_ref, gt_ref, wg_ref, wu_ref, cw_ref, cb_ref, wd_ref, lg_ref, lb_ref,
                o_ref, gbuf):
    tm, d = x_ref.shape
    d_ff = wg_ref.shape[1]
    fw = d_ff // n_split
    s = pl.program_id(1)

    @pl.when(s == 0)
    def _():
        gbuf[0:HALO, :] = jnp.zeros((HALO, d_ff), F32)

    x1 = x_ref[...]
    hb = (x1 * (1.0 + sc_ref[0]) + sh_ref[0]).astype(BF16)
    acc = None
    for j in range(n_split):
        cols = slice(j * fw, (j + 1) * fw)
        gate = jnp.dot(hb, wg_ref[:, cols], preferred_element_type=F32)
        up = jnp.dot(hb, wu_ref[:, cols], preferred_element_type=F32)
        gbuf[HALO:HALO + tm, cols] = gate
        conv = _causal_conv(gbuf.at[:, cols], cw_ref[:, cols], tm, FFN_CONV) + cb_ref[:, cols]
        gbuf[0:HALO, cols] = gbuf[tm:tm + HALO, cols]
        act = (_gelu_tanh(conv) * up).astype(BF16)
        part = jnp.dot(act, wd_ref[cols, :], preferred_element_type=F32)
        acc = part if acc is None else acc + part
    pre = alpha * x1 + (1.0 + gt_ref[0]) * acc
    o_ref[...] = _layer_norm(pre, lg_ref[...], lb_ref[...])


def _ffn(x1, sc, sh, gt, wg, wu, cw, cb, wd, lg, lb, bsz, seq, tm, alpha, n_split):
    m, d = x1.shape
    d_ff = wg.shape[1]
    nst = seq // tm
    row = lambda b, s: (b * nst + s, 0)
    const = lambda b, s: (0, 0)
    mod = lambda b, s: (b, 0, 0)
    return pl.pallas_call(
        functools.partial(_ffn_kernel, alpha, n_split),
        out_shape=jax.ShapeDtypeStruct((m, d), F32),
        grid=(bsz, nst),
        in_specs=[pl.BlockSpec((tm, d), row),
                  pl.BlockSpec((1, 1, d), mod), pl.BlockSpec((1, 1, d), mod), pl.BlockSpec((1, 1, d), mod),
                  pl.BlockSpec((d, d_ff), const), pl.BlockSpec((d, d_ff), const),
                  pl.BlockSpec((FFN_CONV, d_ff), const), pl.BlockSpec((1, d_ff), const),
                  pl.BlockSpec((d_ff, d), const), pl.BlockSpec((1, d), const), pl.BlockSpec((1, d), const)],
        out_specs=pl.BlockSpec((tm, d), row),
        scratch_shapes=[pltpu.VMEM((tm + HALO, d_ff), F32)],
        compiler_params=_cparams(("arbitrary", "arbitrary")),
        name="ffn",
    )(x1, sc, sh, gt, wg, wu, cw, cb, wd, lg, lb)


def _block_diag(w):
    n, bi, bj = w.shape
    eye = jnp.eye(n, dtype=w.dtype)
    return (eye[:, None, :, None] * w[:, :, None, :]).reshape(n * bi, n * bj)


def _prep_rg_gates(w_a, w_x, k_starts):
    da, dx = _block_diag(w_a), _block_diag(w_x)
    tiles = []
    for j, k0 in enumerate(k_starts):
        cols = slice(j * RG_TILE_N, (j + 1) * RG_TILE_N)
        tiles.append(jnp.concatenate([da[k0:k0 + RG_TILE_K, cols], dx[k0:k0 + RG_TILE_K, cols]], axis=1))
    return jnp.stack(tiles).astype(BF16)


def kernel(x, c, w_ada, b_ada, w_in, rg_conv_w, rg_conv_b, rg_w_a, rg_b_a, rg_w_x, rg_b_x, rg_lambda, dn_conv_w, dn_a_log, dn_dt_bias, dn_norm_w, w_proj_a, w_proj_b, w_out, ln1_g, ln1_b, ffn_w_gate, ffn_w_up, ffn_conv_w, ffn_conv_b, ffn_w_down, ln2_g, ln2_b):
    bsz, seq, d = x.shape
    depth = w_ada.shape[0]
    m = bsz * seq
    d_rnn = rg_conv_w.shape[2]
    n_heads = dn_a_log.shape[1]
    dn_v = n_heads * DN_DV
    dn_qk = (dn_conv_w.shape[2] - dn_v) // 2
    alpha = (2 * depth) ** 0.25
    k_starts = _rg_windows(d_rnn)

    col_gr = d_rnn
    col_q = 2 * d_rnn
    col_k = col_q + dn_qk
    col_v = col_k + dn_qk
    col_z = col_v + dn_v
    col_ab_src = col_z + dn_v
    col_ga = col_ab_src
    col_gb = col_ga + d
    col_ab = col_gb + d
    assert col_gr % d_rnn == 0

    tm_in = min(1024, seq)
    tn_in = (col_ab + AB_PAD) // 5 if (col_ab + AB_PAD) % (5 * LANES) == 0 else LANES
    ts_rg = min(256, seq)
    tt_gdn = min(256, seq)
    tm_merge = min(512, seq)
    tm_ffn = min(512, seq)

    x2 = x.reshape(m, d)
    for l in range(depth):
        ada = _adaln(c, w_ada[l], b_ada[l])
        sh1, sc1, gt1, sh2, sc2, gt2 = [t.reshape(bsz, 1, d) for t in jnp.split(ada, 6, axis=-1)]

        wi = w_in[l]
        w_r = jnp.concatenate([wi[:, :col_ab_src], wi[:, col_ab_src + 2 * n_heads:],
                               wi[:, col_ab_src:col_ab_src + 2 * n_heads],
                               jnp.zeros((d, AB_PAD - 2 * n_heads), wi.dtype)], axis=1).astype(BF16)
        p = _inproj(x2, sc1, sh1, w_r, seq, tm_in, tn_in)

        wg = _prep_rg_gates(rg_w_a[l], rg_w_x[l], k_starts)
        ya = _rglru(p, rg_conv_w[l], rg_conv_b[l].reshape(1, d_rnn), wg, rg_b_a[l].reshape(1, d_rnn),
                    rg_b_x[l].reshape(1, d_rnn), rg_lambda[l].reshape(1, d_rnn), w_proj_a[l].astype(BF16),
                    bsz, seq, ts_rg, k_starts)

        pad_h = lambda v: jnp.zeros((1, LANES), F32).at[0, :n_heads].set(v)
        yb = _gdn(p.reshape(bsz, seq, -1), dn_conv_w[l], pad_h(dn_a_log[l]), pad_h(dn_dt_bias[l]),
                  dn_norm_w[l].reshape(1, DN_DV), w_proj_b[l].astype(BF16),
                  col_q, col_k, col_v, col_z, col_ab, n_heads, tt_gdn).reshape(m, d)

        x1 = _merge(x2, ya, yb, p, gt1, w_out[l].astype(BF16), ln1_g[l].reshape(1, d), ln1_b[l].reshape(1, d),
                    col_ga, col_gb, seq, tm_merge, alpha)

        d_ff = ffn_w_gate.shape[2]
        x2 = _ffn(x1, sc2, sh2, gt2, ffn_w_gate[l].astype(BF16), ffn_w_up[l].astype(BF16), ffn_conv_w[l],
                  ffn_conv_b[l].reshape(1, d_ff), ffn_w_down[l].astype(BF16), ln2_g[l].reshape(1, d),
                  ln2_b[l].reshape(1, d), bsz, seq, tm_ffn, alpha, 2)
    return x2.reshape(bsz, seq, d)
```

```python
import functools

import jax
import jax.numpy as jnp
from jax import lax
from jax.experimental import pallas as pl
from jax.experimental.pallas import tpu as pltpu

F32 = jnp.float32
BF16 = jnp.bfloat16

LANES = 128
SUBLANES = 8
VMEM_LIMIT_BYTES = 56 * 1024 * 1024

RG_BLOCKS = 16
RG_CONV = 4
RG_C = 8.0
DN_DK = 128
DN_DV = 128
DN_CONV = 4
CHUNK = 64
FFN_CONV = 3
LN_EPS = 1e-5
RMS_EPS = 1e-6
L2_EPS = 1e-6
HALO = SUBLANES
AB_PAD = LANES


def _cparams(sem):
    return pltpu.CompilerParams(dimension_semantics=sem, vmem_limit_bytes=VMEM_LIMIT_BYTES)


def _resident(shape):
    return pl.BlockSpec(shape, lambda *_: (0,) * len(shape), pipeline_mode=pl.Buffered(1))


def _sigmoid(x):
    return 1.0 / (1.0 + jnp.exp(-x))


def _silu(x):
    return x * _sigmoid(x)


def _softplus(x):
    return jnp.maximum(x, 0.0) + jnp.log(1.0 + jnp.exp(-jnp.abs(x)))


def _gelu_tanh(x):
    return 0.5 * x * (1.0 + jnp.tanh(0.7978845608028654 * (x + 0.044715 * (x * x * x))))


def _layer_norm(x, g, b):
    mu = jnp.mean(x, axis=-1, keepdims=True)
    xc = x - mu
    var = jnp.mean(xc * xc, axis=-1, keepdims=True)
    return xc * lax.rsqrt(var + LN_EPS) * g + b


def _causal_conv(buf_ref, w, n_rows, width):
    acc = None
    for k in range(width):
        off = HALO - (width - 1) + k
        term = w[k:k + 1, :] * buf_ref[off:off + n_rows, :]
        acc = term if acc is None else acc + term
    return acc


def _adaln_kernel(c_ref, w_ref, b_ref, o_ref):
    c = c_ref[...]
    o_ref[...] = jnp.dot(_silu(c), w_ref[...], preferred_element_type=F32,
                         precision=lax.Precision.HIGHEST) + b_ref[...]


def _adaln(c, w_ada, b_ada):
    bsz, d = c.shape
    n = w_ada.shape[1]
    rows = -(-bsz // SUBLANES) * SUBLANES
    c_pad = jnp.zeros((rows, d), F32).at[:bsz].set(c)
    tn = 1536
    out = pl.pallas_call(
        _adaln_kernel,
        out_shape=jax.ShapeDtypeStruct((rows, n), F32),
        grid=(n // tn,),
        in_specs=[pl.BlockSpec((rows, d), lambda j: (0, 0)),
                  pl.BlockSpec((d, tn), lambda j: (0, j)),
                  pl.BlockSpec((1, tn), lambda j: (0, j))],
        out_specs=pl.BlockSpec((rows, tn), lambda j: (0, j)),
        compiler_params=_cparams(("arbitrary",)),
        name="adaln",
    )(c_pad, w_ada, b_ada.reshape(1, n))
    return out[:bsz]


def _inproj_kernel(x_ref, sc_ref, sh_ref, w_ref, o_ref, h_scr):
    @pl.when(pl.program_id(1) == 0)
    def _():
        h = x_ref[...] * (1.0 + sc_ref[0]) + sh_ref[0]
        h_scr[...] = h.astype(BF16)

    o_ref[...] = jnp.dot(h_scr[...], w_ref[...], preferred_element_type=F32)


def _inproj(x2, sc, sh, w, seq, tm, tn):
    m, d = x2.shape
    n = w.shape[1]
    return pl.pallas_call(
        _inproj_kernel,
        out_shape=jax.ShapeDtypeStruct((m, n), F32),
        grid=(m // tm, n // tn),
        in_specs=[pl.BlockSpec((tm, d), lambda i, j: (i, 0)),
                  pl.BlockSpec((1, 1, d), lambda i, j: ((i * tm) // seq, 0, 0)),
                  pl.BlockSpec((1, 1, d), lambda i, j: ((i * tm) // seq, 0, 0)),
                  pl.BlockSpec((d, tn), lambda i, j: (0, j))],
        out_specs=pl.BlockSpec((tm, tn), lambda i, j: (i, j)),
        scratch_shapes=[pltpu.VMEM((tm, d), BF16)],
        compiler_params=_cparams(("arbitrary", "arbitrary")),
        name="inproj",
    )(x2, sc, sh, w)


RG_TILE_N = 256
RG_TILE_K = 512


def _rg_windows(d_rnn):
    bw = d_rnn // RG_BLOCKS
    starts = []
    for j in range(d_rnn // RG_TILE_N):
        lo = (j * RG_TILE_N) // bw * bw
        hi = -(-((j + 1) * RG_TILE_N) // bw) * bw
        k0 = min(lo // LANES * LANES, d_rnn - RG_TILE_K)
        assert k0 <= lo and hi <= k0 + RG_TILE_K
        starts.append(k0)
    return tuple(starts)


def _rglru_kernel(k_starts, xr_ref, gr_ref, cw_ref, cb_ref, wg_ref, ba_ref, bx_ref, lam_ref, wp_ref,
                  o_ref, xbuf, a_scr, u_scr, h_scr):
    ts, d_rnn = xr_ref.shape
    s = pl.program_id(1)

    @pl.when(s == 0)
    def _():
        xbuf[0:HALO, :] = jnp.zeros((HALO, d_rnn), F32)
        h_scr[...] = jnp.zeros_like(h_scr)

    xbuf[HALO:HALO + ts, :] = xr_ref[...]
    xc = _causal_conv(xbuf, cw_ref[...], ts, RG_CONV) + cb_ref[...]
    xbuf[0:HALO, :] = xbuf[ts:ts + HALO, :]

    xcb = xc.astype(BF16)
    pre_r, pre_i = [], []
    for j, k0 in enumerate(k_starts):
        r = jnp.dot(xcb[:, k0:k0 + RG_TILE_K], wg_ref[j], preferred_element_type=F32)
        pre_r.append(r[:, :RG_TILE_N])
        pre_i.append(r[:, RG_TILE_N:])
    gate_r = _sigmoid(jnp.concatenate(pre_r, axis=1) + ba_ref[...])
    gate_i = _sigmoid(jnp.concatenate(pre_i, axis=1) + bx_ref[...])
    log_a = (-RG_C) * gate_r * _softplus(-lam_ref[...])
    a = jnp.exp(log_a)
    a_scr[...] = a
    u_scr[...] = jnp.sqrt(1.0 - jnp.exp(2.0 * log_a)) * gate_i * xc

    row = lax.broadcasted_iota(jnp.int32, (SUBLANES, d_rnn), 0)

    def group(g, h_prev):
        r0 = pl.multiple_of(g * SUBLANES, SUBLANES)
        ag = a_scr[pl.ds(r0, SUBLANES), :]
        ug = u_scr[pl.ds(r0, SUBLANES), :]
        for d in (1, 2, 4):
            keep = row >= d
            a_sh = jnp.where(keep, pltpu.roll(ag, d, axis=0), 1.0)
            u_sh = jnp.where(keep, pltpu.roll(ug, d, axis=0), 0.0)
            ug = ug + ag * u_sh
            ag = ag * a_sh
        hg = ug + ag * h_prev
        u_scr[pl.ds(r0, SUBLANES), :] = hg
        return jnp.broadcast_to(hg[SUBLANES - 1:SUBLANES, :], (SUBLANES, d_rnn))

    h_scr[...] = lax.fori_loop(0, ts // SUBLANES, group, h_scr[...], unroll=4)

    rec = u_scr[...] * _gelu_tanh(gr_ref[...])
    o_ref[...] = jnp.dot(rec.astype(BF16), wp_ref[...], preferred_element_type=F32).astype(o_ref.dtype)


def _rglru(p, cw, cb, wg, ba, bx, lam, wp, bsz, seq, ts, k_starts):
    d_rnn = cw.shape[1]
    d = wp.shape[1]
    nst = seq // ts
    row_map = lambda b, s: (b * nst + s, 0)
    return pl.pallas_call(
        functools.partial(_rglru_kernel, k_starts),
        out_shape=jax.ShapeDtypeStruct((bsz * seq, d), BF16),
        grid=(bsz, nst),
        in_specs=[pl.BlockSpec((ts, d_rnn), lambda b, s: (b * nst + s, 0)),
                  pl.BlockSpec((ts, d_rnn), lambda b, s: (b * nst + s, 1)),
                  _resident(cw.shape), _resident((1, d_rnn)), _resident(wg.shape), _resident((1, d_rnn)),
                  _resident((1, d_rnn)), _resident((1, d_rnn)), _resident(wp.shape)],
        out_specs=pl.BlockSpec((ts, d), row_map),
        scratch_shapes=[pltpu.VMEM((ts + HALO, d_rnn), F32),
                        pltpu.VMEM((ts, d_rnn), F32),
                        pltpu.VMEM((ts, d_rnn), F32),
                        pltpu.VMEM((SUBLANES, d_rnn), F32)],
        compiler_params=_cparams(("arbitrary", "arbitrary")),
        name="rglru",
    )(p, p, cw, cb, wg, ba, bx, lam, wp)


def _chunk_cumsum(x):
    rows = x.shape[0]
    row = lax.broadcasted_iota(jnp.int32, x.shape, 0) % CHUNK
    d = 1
    while d < CHUNK:
        x = x + jnp.where(row >= d, pltpu.roll(x, d, axis=0), 0.0)
        d *= 2
    del rows
    return x


def _pair_blockdiag(m2, left):
    return jnp.concatenate([jnp.where(left, m2, 0.0), jnp.where(left, 0.0, m2)], axis=0)


def _gdn_kernel(q_ref, k_ref, v_ref, z_ref, ab_ref, cwq_ref, cwk_ref, cwv_ref, alog_ref, dtb_ref, nw_ref, wb_ref,
                o_ref,
                s_scr, hq_scr, hk_scr, hv_scr, cq, ck, cv,
                u_scr, w_scr, qd_scr, qk_scr, kdt_scr, gl_scr, dn_scr):
    bsz, tt, _ = q_ref.shape
    n_chunks = tt // CHUNK
    s = pl.program_id(0)
    hg = pl.program_id(1)

    @pl.when(s == 0)
    def _():
        hq_scr[hg] = jnp.zeros(hq_scr.shape[1:], F32)
        hk_scr[hg] = jnp.zeros(hk_scr.shape[1:], F32)
        hv_scr[hg] = jnp.zeros(hv_scr.shape[1:], F32)
        s_scr[2 * hg] = jnp.zeros(s_scr.shape[1:], F32)
        s_scr[2 * hg + 1] = jnp.zeros(s_scr.shape[1:], F32)

    lane = lax.broadcasted_iota(jnp.int32, (CHUNK, LANES), 1)
    row = lax.broadcasted_iota(jnp.int32, (CHUNK, LANES), 0)
    left = lane < CHUNK
    jj = lane % CHUNK
    causal = row >= jj
    strict = row > jj
    eye2 = jnp.where(row == jj, 1.0, 0.0)
    neg_a = -jnp.exp(alog_ref[...])
    dtb = dtb_ref[...]

    def lane_col(arr, col):
        r = pltpu.roll(arr, (LANES - col) % LANES, axis=1)
        return jnp.broadcast_to(r[:, 0:1], arr.shape)

    nb = cq.shape[0]
    n_heads = s_scr.shape[0]
    n_steps = CHUNK.bit_length() - 2

    def prep(bi, carry):
        chains = []
        for t in range(nb):
            b = bi * nb + t

            def conv(x_ref, halo_scr, buf, cw_ref):
                buf[t, 0:HALO, :] = halo_scr[hg, b]
                buf[t, HALO:HALO + tt, :] = x_ref[b]
                halo_scr[hg, b] = buf[t, tt:tt + HALO, :]
                return _silu(_causal_conv(buf.at[t], cw_ref[...], tt, DN_CONV))

            qc = conv(q_ref, hq_scr, cq, cwq_ref)
            kc = conv(k_ref, hk_scr, ck, cwk_ref)
            vc = conv(v_ref, hv_scr, cv, cwv_ref)
            qn = qc * (lax.rsqrt(jnp.sum(qc * qc, axis=-1, keepdims=True) + L2_EPS) * (DN_DK ** -0.5))
            kn = kc * lax.rsqrt(jnp.sum(kc * kc, axis=-1, keepdims=True) + L2_EPS)
            ab = ab_ref[b]
            g_all = neg_a * _softplus(ab + dtb)
            beta_all = _sigmoid(ab)
            gb = [_chunk_cumsum(lane_col(g_all, 2 * hg + j)) for j in range(2)]
            bb = [lane_col(beta_all, n_heads + 2 * hg + j) for j in range(2)]
            for c in range(n_chunks):
                rs = slice(c * CHUNK, (c + 1) * CHUNK)
                chains.append(dict(b=b, c=c, qn=qn[rs], kn=kn[rs], v=vc[rs], g=(gb[0][rs], gb[1][rs]),
                                   beta=(bb[0][rs], bb[1][rs])))

        for ch in chains:
            kn_b = ch["kn"].astype(BF16)
            qk_lhs = jnp.concatenate([ch["qn"].astype(BF16), kn_b], axis=0)
            kk_rhs = jnp.concatenate([kn_b, kn_b], axis=0)
            ch["kq"] = lax.dot_general(qk_lhs, kk_rhs, (((1,), (1,)), ((), ())), preferred_element_type=F32)
        for ch in chains:
            g0, g1 = ch["g"]
            g_pair = jnp.where(left, g0, g1)
            b_pair = jnp.where(left, ch["beta"][0], ch["beta"][1])
            g_rows = jnp.transpose(jnp.concatenate([g0, g1], axis=0))[0:CHUNK]
            decay = jnp.where(causal, jnp.exp(jnp.where(causal, g_pair - g_rows, 0.0)), 0.0)
            kq = ch.pop("kq")
            ch["qk2"] = (kq[0:CHUNK] * decay).astype(BF16)
            m = -jnp.where(strict, b_pair * kq[CHUNK:2 * CHUNK] * decay, 0.0)
            ch["qsum"] = eye2 + m
            ch["m"] = jnp.dot(m.astype(BF16), _pair_blockdiag(m, left).astype(BF16), preferred_element_type=F32)
        for it in range(n_steps):
            for ch in chains:
                bd = _pair_blockdiag(ch["m"], left).astype(BF16)
                if it < n_steps - 1:
                    r = jnp.dot(jnp.concatenate([ch["qsum"], ch["m"]], axis=0).astype(BF16), bd,
                                preferred_element_type=F32)
                    ch["qsum"] = ch["qsum"] + r[0:CHUNK]
                    ch["m"] = r[CHUNK:2 * CHUNK]
                else:
                    ch["qsum"] = ch["qsum"] + jnp.dot(ch["qsum"].astype(BF16), bd, preferred_element_type=F32)
        for ch in chains:
            (g0, g1), (b0, b1), kn_c = ch["g"], ch["beta"], ch["kn"]
            eg = (jnp.exp(g0), jnp.exp(g1))
            v0, v1 = ch["v"][:, 0:DN_DV], ch["v"][:, DN_DV:2 * DN_DV]
            rhs = jnp.concatenate([jnp.concatenate([b0 * v0, (b0 * eg[0]) * kn_c], axis=1),
                                   jnp.concatenate([b1 * v1, (b1 * eg[1]) * kn_c], axis=1)], axis=0)
            ch["uw"] = jnp.dot(_pair_blockdiag(ch["qsum"], left).astype(BF16), rhs.astype(BF16),
                               preferred_element_type=F32)
            ch["eg"] = eg
        for ch in chains:
            b, c, (g0, g1), kn_c, uw = ch["b"], ch["c"], ch["g"], ch["kn"], ch["uw"]
            gl = (jnp.broadcast_to(g0[CHUNK - 1:CHUNK], g0.shape), jnp.broadcast_to(g1[CHUNK - 1:CHUNK], g1.shape))
            kd = jnp.concatenate([kn_c * jnp.exp(gl[0] - g0), kn_c * jnp.exp(gl[1] - g1)], axis=0)
            kdt_scr[b, c] = jnp.transpose(kd).astype(BF16)
            qk_scr[b, c] = ch["qk2"]
            for j in range(2):
                u_scr[j, b, c] = uw[j * CHUNK:(j + 1) * CHUNK, 0:DN_DV]
                w_scr[j, b, c] = uw[j * CHUNK:(j + 1) * CHUNK, DN_DV:2 * DN_DV].astype(BF16)
                qd_scr[j, b, c] = (ch["qn"] * ch["eg"][j]).astype(BF16)
                gl_scr[j, b, c] = jnp.exp(gl[j][0:SUBLANES])
        return carry

    lax.fori_loop(0, bsz // nb, prep, 0)

    nw = nw_ref[...]
    zeros_v = jnp.zeros((CHUNK, DN_DV), F32)
    for c in range(n_chunks):
        st, r1 = {}, {}
        for b in range(bsz):
            for j in range(2):
                st[b, j] = s_scr[2 * hg + j, b]
                lhs = jnp.concatenate([w_scr[j, b, c], qd_scr[j, b, c]], axis=0)
                r1[b, j] = jnp.dot(lhs, st[b, j].astype(BF16), preferred_element_type=F32)
        r2 = {}
        for b in range(bsz):
            vnew = [u_scr[j, b, c] - r1[b, j][0:CHUNK] for j in range(2)]
            bdv = jnp.concatenate([jnp.concatenate([vnew[0], zeros_v], axis=1),
                                   jnp.concatenate([zeros_v, vnew[1]], axis=1)], axis=0).astype(BF16)
            lhs2 = jnp.concatenate([qk_scr[b, c], kdt_scr[b, c]], axis=0)
            r2[b] = jnp.dot(lhs2, bdv, preferred_element_type=F32)
        for b in range(bsz):
            outs = []
            for j in range(2):
                cols = slice(j * DN_DV, (j + 1) * DN_DV)
                o = r1[b, j][CHUNK:2 * CHUNK] + r2[b][0:CHUNK, cols]
                s_scr[2 * hg + j, b] = gl_scr[j, b, c][0:1, :] * st[b, j] + r2[b][CHUNK:CHUNK + DN_DK, cols]
                o = o * lax.rsqrt(jnp.mean(o * o, axis=-1, keepdims=True) + RMS_EPS) * nw
                outs.append(o)
            zc = z_ref[b, c * CHUNK:(c + 1) * CHUNK, :]
            dn_scr[hg, b, c * CHUNK:(c + 1) * CHUNK, :] = (jnp.concatenate(outs, axis=1) * _silu(zc)).astype(BF16)

    @pl.when(hg == pl.num_programs(1) - 1)
    def _():
        for b in range(bsz):
            dn = jnp.concatenate([dn_scr[g, b] for g in range(dn_scr.shape[0])], axis=1)
            o_ref[b] = jnp.dot(dn, wb_ref[...], preferred_element_type=F32).astype(o_ref.dtype)


def _gdn(p3, cw, alog, dtb, nw, wb, col_q, col_k, col_v, col_z, col_ab, n_heads, tt, nb):
    bsz, seq, _ = p3.shape
    d = wb.shape[1]
    n_groups = n_heads // 2
    nst = seq // tt
    nch = tt // CHUNK
    qk_w = n_groups * DN_DK
    assert col_q % DN_DK == 0 and col_k % DN_DK == 0 and col_v % (2 * DN_DV) == 0 and col_z % (2 * DN_DV) == 0
    in_specs = [
        pl.BlockSpec((bsz, tt, DN_DK), lambda s, g: (0, s, col_q // DN_DK + g)),
        pl.BlockSpec((bsz, tt, DN_DK), lambda s, g: (0, s, col_k // DN_DK + g)),
        pl.BlockSpec((bsz, tt, 2 * DN_DV), lambda s, g: (0, s, col_v // (2 * DN_DV) + g)),
        pl.BlockSpec((bsz, tt, 2 * DN_DV), lambda s, g: (0, s, col_z // (2 * DN_DV) + g)),
        pl.BlockSpec((bsz, tt, AB_PAD), lambda s, g: (0, s, col_ab // AB_PAD)),
        pl.BlockSpec((DN_CONV, DN_DK), lambda s, g: (0, g)),
        pl.BlockSpec((DN_CONV, DN_DK), lambda s, g: (0, qk_w // DN_DK + g)),
        pl.BlockSpec((DN_CONV, 2 * DN_DV), lambda s, g: (0, 2 * qk_w // (2 * DN_DV) + g)),
        _resident((1, LANES)), _resident((1, LANES)), _resident((1, DN_DV)), _resident(wb.shape),
    ]
    scratch = [
        pltpu.VMEM((n_heads, bsz, DN_DK, DN_DV), F32),
        pltpu.VMEM((n_groups, bsz, HALO, DN_DK), F32),
        pltpu.VMEM((n_groups, bsz, HALO, DN_DK), F32),
        pltpu.VMEM((n_groups, bsz, HALO, 2 * DN_DV), F32),
        pltpu.VMEM((nb, tt + HALO, DN_DK), F32),
        pltpu.VMEM((nb, tt + HALO, DN_DK), F32),
        pltpu.VMEM((nb, tt + HALO, 2 * DN_DV), F32),
        pltpu.VMEM((2, bsz, nch, CHUNK, DN_DV), F32),
        pltpu.VMEM((2, bsz, nch, CHUNK, DN_DK), BF16),
        pltpu.VMEM((2, bsz, nch, CHUNK, DN_DK), BF16),
        pltpu.VMEM((bsz, nch, CHUNK, 2 * CHUNK), BF16),
        pltpu.VMEM((bsz, nch, DN_DK, 2 * CHUNK), BF16),
        pltpu.VMEM((2, bsz, nch, SUBLANES, DN_DV), F32),
        pltpu.VMEM((n_groups, bsz, tt, 2 * DN_DV), BF16),
    ]
    return pl.pallas_call(
        _gdn_kernel,
        out_shape=jax.ShapeDtypeStruct((bsz, seq, d), BF16),
        grid=(nst, n_groups),
        in_specs=in_specs,
        out_specs=pl.BlockSpec((bsz, tt, d), lambda s, g: (0, s, 0)),
        scratch_shapes=scratch,
        compiler_params=_cparams(("arbitrary", "arbitrary")),
        name="gdn",
    )(p3, p3, p3, p3, p3, cw, cw, cw, alog, dtb, nw, wb)


def _tail_kernel(alpha, n_split, x_ref, ya_ref, yb_ref, ga0_ref, ga1_ref, gb0_ref, gb1_ref, gt1_ref, wo_ref,
                 l1g_ref, l1b_ref, sc_ref, sh_ref, gt_ref, wg_ref, wu_ref, cw_ref, cb_ref, wd_ref, lg_ref, lb_ref,
                 o_ref, gbuf):
    tm, d = x_ref.shape
    d_ff = wg_ref.shape[1]
    fw = d_ff // n_split
    s = pl.program_id(1)

    @pl.when(s == 0)
    def _():
        gbuf[0:HALO, :] = jnp.zeros((HALO, d_ff), F32)

    ga = jnp.concatenate([ga0_ref[...], ga1_ref[...]], axis=1)
    gb = jnp.concatenate([gb0_ref[...], gb1_ref[...]], axis=1)
    merged = _sigmoid(ga) * ya_ref[...].astype(F32) + _sigmoid(gb) * yb_ref[...].astype(F32)
    mix = jnp.dot(merged.astype(BF16), wo_ref[...], preferred_element_type=F32)
    x1 = _layer_norm(alpha * x_ref[...] + (1.0 + gt1_ref[0]) * mix, l1g_ref[...], l1b_ref[...])

    hb = (x1 * (1.0 + sc_ref[0]) + sh_ref[0]).astype(BF16)
    acc = None
    for j in range(n_split):
        cols = slice(j * fw, (j + 1) * fw)
        gate = jnp.dot(hb, wg_ref[:, cols], preferred_element_type=F32)
        up = jnp.dot(hb, wu_ref[:, cols], preferred_element_type=F32)
        gbuf[HALO:HALO + tm, cols] = gate
        conv = _causal_conv(gbuf.at[:, cols], cw_ref[:, cols], tm, FFN_CONV) + cb_ref[:, cols]
        gbuf[0:HALO, cols] = gbuf[tm:tm + HALO, cols]
        act = (_gelu_tanh(conv) * up).astype(BF16)
        part = jnp.dot(act, wd_ref[cols, :], preferred_element_type=F32)
        acc = part if acc is None else acc + part
    pre = alpha * x1 + (1.0 + gt_ref[0]) * acc
    o_ref[...] = _layer_norm(pre, lg_ref[...], lb_ref[...])


def _tail(x2, ya, yb, p, gt1, wo, l1g, l1b, sc, sh, gt, wg, wu, cw, cb, wd, lg, lb, col_ga, col_gb,
          bsz, seq, tm, alpha, n_split):
    m, d = x2.shape
    d_ff = wg.shape[1]
    nst = seq // tm
    half = d // 2
    assert col_ga % half == 0 and col_gb % half == 0
    row = lambda b, s: (b * nst + s, 0)
    mod = lambda b, s: (b, 0, 0)
    pcol = lambda blk: pl.BlockSpec((tm, half), lambda b, s: (b * nst + s, blk))
    return pl.pallas_call(
        functools.partial(_tail_kernel, alpha, n_split),
        out_shape=jax.ShapeDtypeStruct((m, d), F32),
        grid=(bsz, nst),
        in_specs=[pl.BlockSpec((tm, d), row), pl.BlockSpec((tm, d), row), pl.BlockSpec((tm, d), row),
                  pcol(col_ga // half), pcol(col_ga // half + 1), pcol(col_gb // half), pcol(col_gb // half + 1),
                  pl.BlockSpec((1, 1, d), mod), _resident((d, d)), _resident((1, d)), _resident((1, d)),
                  pl.BlockSpec((1, 1, d), mod), pl.BlockSpec((1, 1, d), mod), pl.BlockSpec((1, 1, d), mod),
                  _resident((d, d_ff)), _resident((d, d_ff)), _resident((FFN_CONV, d_ff)), _resident((1, d_ff)),
                  _resident((d_ff, d)), _resident((1, d)), _resident((1, d))],
        out_specs=pl.BlockSpec((tm, d), row),
        scratch_shapes=[pltpu.VMEM((tm + HALO, d_ff), F32)],
        compiler_params=_cparams(("arbitrary", "arbitrary")),
        name="tail",
    )(x2, ya, yb, p, p, p, p, gt1, wo, l1g, l1b, sc, sh, gt, wg, wu, cw, cb, wd, lg, lb)


def _block_diag(w):
    n, bi, bj = w.shape
    eye = jnp.eye(n, dtype=w.dtype)
    return (eye[:, None, :, None] * w[:, :, None, :]).reshape(n * bi, n * bj)


def _prep_rg_gates(w_a, w_x, k_starts):
    da, dx = _block_diag(w_a), _block_diag(w_x)
    tiles = []
    for j, k0 in enumerate(k_starts):
        cols = slice(j * RG_TILE_N, (j + 1) * RG_TILE_N)
        tiles.append(jnp.concatenate([da[k0:k0 + RG_TILE_K, cols], dx[k0:k0 + RG_TILE_K, cols]], axis=1))
    return jnp.stack(tiles).astype(BF16)


def kernel(x, c, w_ada, b_ada, w_in, rg_conv_w, rg_conv_b, rg_w_a, rg_b_a, rg_w_x, rg_b_x, rg_lambda, dn_conv_w, dn_a_log, dn_dt_bias, dn_norm_w, w_proj_a, w_proj_b, w_out, ln1_g, ln1_b, ffn_w_gate, ffn_w_up, ffn_conv_w, ffn_conv_b, ffn_w_down, ln2_g, ln2_b):
    bsz, seq, d = x.shape
    depth = w_ada.shape[0]
    m = bsz * seq
    d_rnn = rg_conv_w.shape[2]
    n_heads = dn_a_log.shape[1]
    dn_v = n_heads * DN_DV
    dn_qk = (dn_conv_w.shape[2] - dn_v) // 2
    alpha = (2 * depth) ** 0.25
    k_starts = _rg_windows(d_rnn)

    col_gr = d_rnn
    col_q = 2 * d_rnn
    col_k = col_q + dn_qk
    col_v = col_k + dn_qk
    col_z = col_v + dn_v
    col_ab_src = col_z + dn_v
    col_ga = col_ab_src
    col_gb = col_ga + d
    col_ab = col_gb + d
    assert col_gr % d_rnn == 0

    tm_in = min(1024, seq)
    tn_in = (col_ab + AB_PAD) // 5 if (col_ab + AB_PAD) % (5 * LANES) == 0 else LANES
    ts_rg = min(256, seq)
    tt_gdn = min(256, seq)
    nb_gdn = 2 if bsz % 2 == 0 else 1
    tm_tail = min(512, seq)

    x2 = x.reshape(m, d)
    for l in range(depth):
        ada = _adaln(c, w_ada[l], b_ada[l])
        sh1, sc1, gt1, sh2, sc2, gt2 = [t.reshape(bsz, 1, d) for t in jnp.split(ada, 6, axis=-1)]

        wi = w_in[l]
        w_r = jnp.concatenate([wi[:, :col_ab_src], wi[:, col_ab_src + 2 * n_heads:],
                               wi[:, col_ab_src:col_ab_src + 2 * n_heads],
                               jnp.zeros((d, AB_PAD - 2 * n_heads), wi.dtype)], axis=1).astype(BF16)
        p = _inproj(x2, sc1, sh1, w_r, seq, tm_in, tn_in)

        wg = _prep_rg_gates(rg_w_a[l], rg_w_x[l], k_starts)
        ya = _rglru(p, rg_conv_w[l], rg_conv_b[l].reshape(1, d_rnn), wg, rg_b_a[l].reshape(1, d_rnn),
                    rg_b_x[l].reshape(1, d_rnn), rg_lambda[l].reshape(1, d_rnn), w_proj_a[l].astype(BF16),
                    bsz, seq, ts_rg, k_starts)

        pad_h = lambda v: jnp.zeros((1, LANES), F32).at[0, :n_heads].set(v)
        yb = _gdn(p.reshape(bsz, seq, -1), dn_conv_w[l], pad_h(dn_a_log[l]), pad_h(dn_dt_bias[l]),
                  dn_norm_w[l].reshape(1, DN_DV), w_proj_b[l].astype(BF16),
                  col_q, col_k, col_v, col_z, col_ab, n_heads, tt_gdn, nb_gdn).reshape(m, d)

        d_ff = ffn_w_gate.shape[2]
        x2 = _tail(x2, ya, yb, p, gt1, w_out[l].astype(BF16), ln1_g[l].reshape(1, d), ln1_b[l].reshape(1, d),
                   sc2, sh2, gt2, ffn_w_gate[l].astype(BF16), ffn_w_up[l].astype(BF16), ffn_conv_w[l],
                   ffn_conv_b[l].reshape(1, d_ff), ffn_w_down[l].astype(BF16), ln2_g[l].reshape(1, d),
                   ln2_b[l].reshape(1, d), col_ga, col_gb, bsz, seq, tm_tail, alpha, 2)
    return x2.reshape(bsz, seq, d)
```

```python
import functools

import jax
import jax.numpy as jnp
from jax import lax
from jax.experimental import pallas as pl
from jax.experimental.pallas import tpu as pltpu

F32 = jnp.float32
BF16 = jnp.bfloat16

LANES = 128
SUBLANES = 8
VMEM_LIMIT_BYTES = 56 * 1024 * 1024

RG_BLOCKS = 16
RG_CONV = 4
RG_C = 8.0
DN_DK = 128
DN_DV = 128
DN_CONV = 4
CHUNK = 64
FFN_CONV = 3
LN_EPS = 1e-5
RMS_EPS = 1e-6
L2_EPS = 1e-6
HALO = SUBLANES
AB_PAD = LANES
GROUP_W = 2 * DN_DK + 4 * DN_DV
CONV_ROWS = 128


def _cparams(sem):
    return pltpu.CompilerParams(dimension_semantics=sem, vmem_limit_bytes=VMEM_LIMIT_BYTES)


def _resident(shape):
    return pl.BlockSpec(shape, lambda *_: (0,) * len(shape), pipeline_mode=pl.Buffered(1))


def _sigmoid(x):
    return 0.5 * jnp.tanh(0.5 * x) + 0.5


def _silu(x):
    h = 0.5 * x
    return h * jnp.tanh(h) + h


def _softplus(x):
    return jnp.maximum(x, 0.0) + jnp.log(1.0 + jnp.exp(-jnp.abs(x)))


def _gelu_tanh(x):
    h = 0.5 * x
    return h * jnp.tanh(x * (0.7978845608028654 + (0.7978845608028654 * 0.044715) * (x * x))) + h


def _layer_norm(x, g, b):
    mu = jnp.mean(x, axis=-1, keepdims=True)
    xc = x - mu
    var = jnp.mean(xc * xc, axis=-1, keepdims=True)
    return xc * lax.rsqrt(var + LN_EPS) * g + b


def _causal_conv(buf_ref, w, n_rows, width, row0=0):
    acc = None
    for k in range(width):
        off = row0 + HALO - (width - 1) + k
        term = w[k:k + 1, :] * buf_ref[off:off + n_rows, :]
        acc = term if acc is None else acc + term
    return acc


def _chunk_cumsum(x):
    row = lax.broadcasted_iota(jnp.int32, x.shape, 0) % CHUNK
    d = 1
    while d < CHUNK:
        x = x + jnp.where(row >= d, pltpu.roll(x, d, axis=0), 0.0)
        d *= 2
    return x


def _modulated(x_ref, sc_ref, sh_ref):
    return (x_ref[...] * (1.0 + sc_ref[0]) + sh_ref[0]).astype(BF16)


def _adaln_kernel(c_ref, w_ref, b_ref, o_ref):
    c = c_ref[...]
    o_ref[...] = jnp.dot(_silu(c), w_ref[...], preferred_element_type=F32,
                         precision=lax.Precision.HIGHEST) + b_ref[...]


def _adaln(c, w_ada, b_ada):
    bsz, d = c.shape
    n = w_ada.shape[1]
    rows = -(-bsz // SUBLANES) * SUBLANES
    c_pad = jnp.zeros((rows, d), F32).at[:bsz].set(c)
    tn = 1536
    out = pl.pallas_call(
        _adaln_kernel,
        out_shape=jax.ShapeDtypeStruct((rows, n), F32),
        grid=(n // tn,),
        in_specs=[pl.BlockSpec((rows, d), lambda j: (0, 0)),
                  pl.BlockSpec((d, tn), lambda j: (0, j)),
                  pl.BlockSpec((1, tn), lambda j: (0, j))],
        out_specs=pl.BlockSpec((rows, tn), lambda j: (0, j)),
        compiler_params=_cparams(("arbitrary",)),
        name="adaln",
    )(c_pad, w_ada, b_ada.reshape(1, n))
    return out[:bsz]


RG_TILE_N = 256
RG_TILE_K = 512


def _rg_windows(d_rnn):
    bw = d_rnn // RG_BLOCKS
    starts = []
    for j in range(d_rnn // RG_TILE_N):
        lo = (j * RG_TILE_N) // bw * bw
        hi = -(-((j + 1) * RG_TILE_N) // bw) * bw
        k0 = min(lo // LANES * LANES, d_rnn - RG_TILE_K)
        assert k0 <= lo and hi <= k0 + RG_TILE_K
        starts.append(k0)
    return tuple(starts)


def _mixer_a_kernel(k_starts, n_heads, x_ref, sc_ref, sh_ref, w_ref, cw_ref, cb_ref, wg_ref, ba_ref, bx_ref, lam_ref,
                    wp_ref, alog_ref, dtb_ref, ya_ref, gate_ref, ab_ref, xbuf, a_scr, u_scr, g_scr, h_scr):
    ts = x_ref.shape[0]
    d_rnn = cw_ref.shape[1]
    n_gate = gate_ref.shape[1]
    s = pl.program_id(1)

    @pl.when(s == 0)
    def _():
        xbuf[0:HALO, :] = jnp.zeros((HALO, d_rnn), F32)
        h_scr[...] = jnp.zeros_like(h_scr)

    hb = _modulated(x_ref, sc_ref, sh_ref)
    c_gate = 2 * d_rnn
    c_ab = c_gate + n_gate
    xbuf[HALO:HALO + ts, :] = jnp.dot(hb, w_ref[:, 0:d_rnn], preferred_element_type=F32)
    g_scr[...] = _gelu_tanh(jnp.dot(hb, w_ref[:, d_rnn:c_gate], preferred_element_type=F32))
    gate_ref[...] = _sigmoid(jnp.dot(hb, w_ref[:, c_gate:c_ab], preferred_element_type=F32)).astype(gate_ref.dtype)
    ab = jnp.dot(hb, w_ref[:, c_ab:c_ab + AB_PAD], preferred_element_type=F32)
    log_decay = -jnp.exp(alog_ref[...]) * _softplus(ab + dtb_ref[...])
    ab_lane = lax.broadcasted_iota(jnp.int32, ab.shape, 1)
    ab_ref[...] = jnp.where(ab_lane < n_heads, _chunk_cumsum(log_decay), _sigmoid(ab))

    xc = _causal_conv(xbuf, cw_ref[...], ts, RG_CONV) + cb_ref[...]
    xbuf[0:HALO, :] = xbuf[ts:ts + HALO, :]

    xcb = xc.astype(BF16)
    pre_r, pre_i = [], []
    for j, k0 in enumerate(k_starts):
        r = jnp.dot(xcb[:, k0:k0 + RG_TILE_K], wg_ref[j], preferred_element_type=F32)
        pre_r.append(r[:, :RG_TILE_N])
        pre_i.append(r[:, RG_TILE_N:])
    gate_r = _sigmoid(jnp.concatenate(pre_r, axis=1) + ba_ref[...])
    gate_i = _sigmoid(jnp.concatenate(pre_i, axis=1) + bx_ref[...])
    log_a = (-RG_C) * gate_r * _softplus(-lam_ref[...])
    a_scr[...] = jnp.exp(log_a)
    u_scr[...] = jnp.sqrt(1.0 - jnp.exp(2.0 * log_a)) * gate_i * xc

    row = lax.broadcasted_iota(jnp.int32, (SUBLANES, d_rnn), 0)

    def group(g, h_prev):
        r0 = pl.multiple_of(g * SUBLANES, SUBLANES)
        ag = a_scr[pl.ds(r0, SUBLANES), :]
        ug = u_scr[pl.ds(r0, SUBLANES), :]
        for d in (1, 2, 4):
            keep = row >= d
            a_sh = jnp.where(keep, pltpu.roll(ag, d, axis=0), 1.0)
            u_sh = jnp.where(keep, pltpu.roll(ug, d, axis=0), 0.0)
            ug = ug + ag * u_sh
            ag = ag * a_sh
        hg = ug + ag * h_prev
        u_scr[pl.ds(r0, SUBLANES), :] = hg
        return jnp.broadcast_to(hg[SUBLANES - 1:SUBLANES, :], (SUBLANES, d_rnn))

    h_scr[...] = lax.fori_loop(0, ts // SUBLANES, group, h_scr[...], unroll=4)

    rec = u_scr[...] * g_scr[...]
    ya_ref[...] = jnp.dot(rec.astype(BF16), wp_ref[...], preferred_element_type=F32).astype(ya_ref.dtype)


def _mixer_a(x2, sc, sh, w, cw, cb, wg, ba, bx, lam, wp, alog, dtb, bsz, seq, ts, k_starts, n_gate, n_heads):
    m, d = x2.shape
    d_rnn = cw.shape[1]
    nst = seq // ts
    row = lambda b, s: (b * nst + s, 0)
    mod = lambda b, s: (b, 0, 0)
    return pl.pallas_call(
        functools.partial(_mixer_a_kernel, k_starts, n_heads),
        out_shape=(jax.ShapeDtypeStruct((m, d), BF16),
                   jax.ShapeDtypeStruct((m, n_gate), BF16),
                   jax.ShapeDtypeStruct((m, AB_PAD), F32)),
        grid=(bsz, nst),
        in_specs=[pl.BlockSpec((ts, d), row), pl.BlockSpec((1, 1, d), mod), pl.BlockSpec((1, 1, d), mod),
                  _resident(w.shape), _resident(cw.shape), _resident((1, d_rnn)), _resident(wg.shape),
                  _resident((1, d_rnn)), _resident((1, d_rnn)), _resident((1, d_rnn)), _resident(wp.shape),
                  _resident((1, LANES)), _resident((1, LANES))],
        out_specs=(pl.BlockSpec((ts, d), row), pl.BlockSpec((ts, n_gate), row), pl.BlockSpec((ts, AB_PAD), row)),
        scratch_shapes=[pltpu.VMEM((ts + HALO, d_rnn), F32),
                        pltpu.VMEM((ts, d_rnn), F32),
                        pltpu.VMEM((ts, d_rnn), F32),
                        pltpu.VMEM((ts, d_rnn), F32),
                        pltpu.VMEM((SUBLANES, d_rnn), F32)],
        compiler_params=_cparams(("arbitrary", "arbitrary")),
        name="mixer_a",
    )(x2, sc, sh, w, cw, cb, wg, ba, bx, lam, wp, alog, dtb)


def _qkvz_kernel(x_ref, sc_ref, sh_ref, w_ref, cw_ref, o_ref, cbuf, halo):
    tm = x_ref.shape[0]
    n_groups = o_ref.shape[0]
    conv_w = 2 * DN_DK + 2 * DN_DV
    s = pl.program_id(1)

    @pl.when(s == 0)
    def _():
        halo[...] = jnp.zeros_like(halo)

    hb = _modulated(x_ref, sc_ref, sh_ref)
    for g in range(n_groups):
        c0 = g * GROUP_W
        acc = jnp.dot(hb, w_ref[:, c0:c0 + GROUP_W], preferred_element_type=F32)
        buf = cbuf.at[g % cbuf.shape[0]]
        buf[0:HALO, :] = halo[g]
        buf[HALO:HALO + tm, :] = acc[:, 0:conv_w]
        halo[g] = buf[tm:tm + HALO, :]
        o_ref[g, :, conv_w:] = acc[:, conv_w:].astype(o_ref.dtype)
        cw = cw_ref[:, c0:c0 + conv_w]
        for r0 in range(0, tm, CONV_ROWS):
            o_ref[g, r0:r0 + CONV_ROWS, 0:conv_w] = _causal_conv(buf, cw, CONV_ROWS, DN_CONV, r0).astype(o_ref.dtype)


def _qkvz(x2, sc, sh, w, cw, bsz, seq, tm, n_groups):
    m, d = x2.shape
    nst = seq // tm
    mod = lambda b, s: (b, 0, 0)
    conv_w = 2 * DN_DK + 2 * DN_DV
    return pl.pallas_call(
        _qkvz_kernel,
        out_shape=jax.ShapeDtypeStruct((n_groups, m, GROUP_W), BF16),
        grid=(bsz, nst),
        in_specs=[pl.BlockSpec((tm, d), lambda b, s: (b * nst + s, 0)),
                  pl.BlockSpec((1, 1, d), mod), pl.BlockSpec((1, 1, d), mod),
                  _resident(w.shape), _resident(cw.shape)],
        out_specs=pl.BlockSpec((n_groups, tm, GROUP_W), lambda b, s: (0, b * nst + s, 0)),
        scratch_shapes=[pltpu.VMEM((2, tm + HALO, conv_w), F32),
                        pltpu.VMEM((n_groups, HALO, conv_w), F32)],
        compiler_params=_cparams(("arbitrary", "arbitrary")),
        name="qkvz",
    )(x2, sc, sh, w, cw)


def _pair_blockdiag(m2, left):
    return jnp.concatenate([jnp.where(left, m2, 0.0), jnp.where(left, 0.0, m2)], axis=0)


def _gdn_kernel(n_groups, x_ref, ab_ref, nw_ref, wb_ref,
                o_ref,
                s_scr, u_scr, w_scr, qd_scr, qk_scr, kdt_scr, gl_scr, z_scr, dn_scr):
    bsz, tt, _ = x_ref.shape
    n_chunks = tt // CHUNK
    t = pl.program_id(0)
    hg = t % n_groups
    hg_r = (t + n_groups - 1) % n_groups
    slot, slot_r = t % 2, (t + 1) % 2
    c_k, c_v, c_z = DN_DK, 2 * DN_DK, 2 * DN_DK + 2 * DN_DV

    @pl.when(t == 0)
    def _():
        for ref in (u_scr, w_scr, qd_scr, qk_scr, kdt_scr, gl_scr, z_scr):
            ref[1] = jnp.zeros(ref.shape[1:], ref.dtype)

    @pl.when(t <= n_groups)
    def _():
        s_scr[2 * hg_r] = jnp.zeros(s_scr.shape[1:], F32)
        s_scr[2 * hg_r + 1] = jnp.zeros(s_scr.shape[1:], F32)

    lane = lax.broadcasted_iota(jnp.int32, (CHUNK, LANES), 1)
    row = lax.broadcasted_iota(jnp.int32, (CHUNK, LANES), 0)
    left = lane < CHUNK
    jj = lane % CHUNK
    causal = row >= jj
    strict = row > jj
    eye2 = jnp.where(row == jj, 1.0, 0.0)

    def lane_col(arr, col):
        r = pltpu.roll(arr, (LANES - col) % LANES, axis=1)
        return jnp.broadcast_to(r[:, 0:1], arr.shape)

    n_heads = s_scr.shape[0]
    n_steps = CHUNK.bit_length() - 2
    nw = nw_ref[...]
    zeros_v = jnp.zeros((CHUNK, DN_DV), F32)

    def load_rows(b):
        act = _silu(x_ref[b, :, 0:c_z].astype(F32))
        q, k = act[:, 0:c_k], act[:, c_k:c_v]
        qn = q * (lax.rsqrt(jnp.sum(q * q, axis=-1, keepdims=True) + L2_EPS) * (DN_DK ** -0.5))
        kn = k * lax.rsqrt(jnp.sum(k * k, axis=-1, keepdims=True) + L2_EPS)
        ab = ab_ref[b]
        gb = [lane_col(ab, 2 * hg + j) for j in range(2)]
        bb = [lane_col(ab, n_heads + 2 * hg + j) for j in range(2)]
        return dict(qn=qn, kn=kn, v=act[:, c_v:c_z], gb=gb, bb=bb)

    def start_chain(ch):
        rows, c = ch["rows"], ch["c"]
        rs = slice(c * CHUNK, (c + 1) * CHUNK)
        ch.update(qn=rows["qn"][rs], kn=rows["kn"][rs], v=rows["v"][rs],
                  g=(rows["gb"][0][rs], rows["gb"][1][rs]), beta=(rows["bb"][0][rs], rows["bb"][1][rs]))
        kn_b = ch["kn"].astype(BF16)
        qk_lhs = jnp.concatenate([ch["qn"].astype(BF16), kn_b], axis=0)
        kk_rhs = jnp.concatenate([kn_b, kn_b], axis=0)
        kq = lax.dot_general(qk_lhs, kk_rhs, (((1,), (1,)), ((), ())), preferred_element_type=F32)
        g0, g1 = ch["g"]
        g_pair = jnp.where(left, g0, g1)
        b_pair = jnp.where(left, ch["beta"][0], ch["beta"][1])
        g_rows = jnp.transpose(jnp.concatenate([g0, g1], axis=0))[0:CHUNK]
        decay = jnp.where(causal, jnp.exp(jnp.where(causal, g_pair - g_rows, 0.0)), 0.0)
        ch["qk2"] = (kq[0:CHUNK] * decay).astype(BF16)
        m = -jnp.where(strict, b_pair * kq[CHUNK:2 * CHUNK] * decay, 0.0)
        ch["qsum"] = eye2 + m
        ch["m"] = jnp.dot(m.astype(BF16), _pair_blockdiag(m, left).astype(BF16), preferred_element_type=F32)

    def inverse_step(ch, it):
        bd = _pair_blockdiag(ch["m"], left).astype(BF16)
        if it < n_steps - 1:
            r = jnp.dot(jnp.concatenate([ch["qsum"], ch["m"]], axis=0).astype(BF16), bd, preferred_element_type=F32)
            ch["qsum"] = ch["qsum"] + r[0:CHUNK]
            ch["m"] = r[CHUNK:2 * CHUNK]
        else:
            ch["qsum"] = ch["qsum"] + jnp.dot(ch["qsum"].astype(BF16), bd, preferred_element_type=F32)

    def finish_chain(ch):
        b, c, (g0, g1), (b0, b1), kn_c, qn_c = ch["b"], ch["c"], ch["g"], ch["beta"], ch["kn"], ch["qn"]
        eg = (jnp.exp(g0), jnp.exp(g1))
        v0, v1 = ch["v"][:, 0:DN_DV], ch["v"][:, DN_DV:2 * DN_DV]
        rhs = jnp.concatenate([jnp.concatenate([b0 * v0, (b0 * eg[0]) * kn_c], axis=1),
                               jnp.concatenate([b1 * v1, (b1 * eg[1]) * kn_c], axis=1)], axis=0)
        uw = jnp.dot(_pair_blockdiag(ch["qsum"], left).astype(BF16), rhs.astype(BF16), preferred_element_type=F32)
        gl = (jnp.broadcast_to(g0[CHUNK - 1:CHUNK], g0.shape), jnp.broadcast_to(g1[CHUNK - 1:CHUNK], g1.shape))
        kd = jnp.concatenate([kn_c * jnp.exp(gl[0] - g0), kn_c * jnp.exp(gl[1] - g1)], axis=0)
        kdt_scr[slot, b, c] = jnp.transpose(kd).astype(BF16)
        qk_scr[slot, b, c] = ch["qk2"]
        for j in range(2):
            u_scr[slot, j, b, c] = uw[j * CHUNK:(j + 1) * CHUNK, 0:DN_DV]
            w_scr[slot, j, b, c] = uw[j * CHUNK:(j + 1) * CHUNK, DN_DV:2 * DN_DV].astype(BF16)
            qd_scr[slot, j, b, c] = (qn_c * eg[j]).astype(BF16)
            gl_scr[slot, j, b, c] = jnp.exp(gl[j][0:SUBLANES])
        rs = slice(c * CHUNK, (c + 1) * CHUNK)
        z_scr[slot, b, c] = _silu(x_ref[b, rs, c_z:c_z + 2 * DN_DV].astype(F32)).astype(BF16)

    rec = {}

    def rec_a(c, b):
        for j in range(2):
            rec["st", b, j] = s_scr[2 * hg_r + j, b]
            lhs = jnp.concatenate([w_scr[slot_r, j, b, c], qd_scr[slot_r, j, b, c]], axis=0)
            rec["r1", b, j] = jnp.dot(lhs, rec["st", b, j].astype(BF16), preferred_element_type=F32)

    def rec_b(c, b):
        vnew = [u_scr[slot_r, j, b, c] - rec["r1", b, j][0:CHUNK] for j in range(2)]
        bdv = jnp.concatenate([jnp.concatenate([vnew[0], zeros_v], axis=1),
                               jnp.concatenate([zeros_v, vnew[1]], axis=1)], axis=0).astype(BF16)
        lhs2 = jnp.concatenate([qk_scr[slot_r, b, c], kdt_scr[slot_r, b, c]], axis=0)
        rec["r2", b] = jnp.dot(lhs2, bdv, preferred_element_type=F32)

    def rec_c(c, b):
        outs = []
        for j in range(2):
            cols = slice(j * DN_DV, (j + 1) * DN_DV)
            r1, r2 = rec.pop(("r1", b, j)), rec["r2", b]
            o = r1[CHUNK:2 * CHUNK] + r2[0:CHUNK, cols]
            s_scr[2 * hg_r + j, b] = gl_scr[slot_r, j, b, c][0:1, :] * rec.pop(("st", b, j)) + r2[CHUNK:CHUNK + DN_DK, cols]
            o = o * lax.rsqrt(jnp.mean(o * o, axis=-1, keepdims=True) + RMS_EPS) * nw
            outs.append(o)
        dn_scr[hg_r, b, c * CHUNK:(c + 1) * CHUNK, :] = (jnp.concatenate(outs, axis=1) * z_scr[slot_r, b, c]).astype(BF16)

    def merged(*streams):
        total = max(len(st) for st in streams)
        order = sorted((i * total // len(st), si, i) for si, st in enumerate(streams) for i in range(len(st)))
        for _, si, i in order:
            streams[si][i]()

    def inverse_stream(chs):
        return [functools.partial(inverse_step, ch, it) for it in range(n_steps) for ch in chs]

    def rec_stream(cs):
        return [functools.partial(f, c, b) for c in cs for f in (rec_a, rec_b, rec_c) for b in range(bsz)]

    rows = [{} for _ in range(bsz)]
    chains = [dict(rows=rows[b], b=b, c=c) for c in range(n_chunks) for b in range(bsz)]
    prep = ([functools.partial(lambda b: rows[b].update(load_rows(b)), b) for b in range(bsz)]
            + [functools.partial(start_chain, ch) for ch in chains]
            + inverse_stream(chains)
            + [functools.partial(finish_chain, ch) for ch in chains])
    merged(prep, rec_stream(range(n_chunks)))

    @pl.when((hg_r == n_groups - 1) & (t > 0))
    def _():
        for b in range(bsz):
            dn = jnp.concatenate([dn_scr[g, b] for g in range(dn_scr.shape[0])], axis=1)
            o_ref[b] = jnp.dot(dn, wb_ref[...], preferred_element_type=F32).astype(o_ref.dtype)


def _gdn(xg, ab3, nw, wb, n_heads, tt):
    n_groups, bsz, seq, _ = xg.shape
    d = wb.shape[1]
    nst = seq // tt
    nch = tt // CHUNK
    n_steps = nst * n_groups
    last = n_steps - 1
    in_specs = [
        pl.BlockSpec((None, bsz, tt, GROUP_W),
                     lambda t: (jnp.minimum(t, last) % n_groups, 0, jnp.minimum(t, last) // n_groups, 0)),
        pl.BlockSpec((bsz, tt, AB_PAD), lambda t: (0, jnp.minimum(t, last) // n_groups, 0)),
        _resident((1, DN_DV)), _resident(wb.shape),
    ]
    scratch = [
        pltpu.VMEM((n_heads, bsz, DN_DK, DN_DV), F32),
        pltpu.VMEM((2, 2, bsz, nch, CHUNK, DN_DV), F32),
        pltpu.VMEM((2, 2, bsz, nch, CHUNK, DN_DK), BF16),
        pltpu.VMEM((2, 2, bsz, nch, CHUNK, DN_DK), BF16),
        pltpu.VMEM((2, bsz, nch, CHUNK, 2 * CHUNK), BF16),
        pltpu.VMEM((2, bsz, nch, DN_DK, 2 * CHUNK), BF16),
        pltpu.VMEM((2, 2, bsz, nch, SUBLANES, DN_DV), F32),
        pltpu.VMEM((2, bsz, nch, CHUNK, 2 * DN_DV), BF16),
        pltpu.VMEM((n_groups, bsz, tt, 2 * DN_DV), BF16),
    ]
    return pl.pallas_call(
        functools.partial(_gdn_kernel, n_groups),
        out_shape=jax.ShapeDtypeStruct((bsz, seq, d), BF16),
        grid=(n_steps + 1,),
        in_specs=in_specs,
        out_specs=pl.BlockSpec((bsz, tt, d), lambda t: (0, jnp.maximum(t - 1, 0) // n_groups, 0)),
        scratch_shapes=scratch,
        compiler_params=_cparams(("arbitrary",)),
        name="gdn",
    )(xg, ab3, nw, wb)


def _tail_kernel(alpha, n_split, x_ref, ya_ref, yb_ref, ga_ref, gb_ref, gt1_ref, wo_ref,
                 l1g_ref, l1b_ref, sc_ref, sh_ref, gt_ref, wg_ref, wu_ref, cw_ref, cb_ref, wd_ref, lg_ref, lb_ref,
                 o_ref, gbuf):
    tm, d = x_ref.shape
    d_ff = wg_ref.shape[1]
    fw = d_ff // n_split
    s = pl.program_id(1)

    @pl.when(s == 0)
    def _():
        gbuf[0:HALO, :] = jnp.zeros((HALO, d_ff), F32)

    merged = (ga_ref[...].astype(F32) * ya_ref[...].astype(F32)
              + gb_ref[...].astype(F32) * yb_ref[...].astype(F32))
    mix = jnp.dot(merged.astype(BF16), wo_ref[...], preferred_element_type=F32)
    x1 = _layer_norm(alpha * x_ref[...] + (1.0 + gt1_ref[0]) * mix, l1g_ref[...], l1b_ref[...])

    hb = (x1 * (1.0 + sc_ref[0]) + sh_ref[0]).astype(BF16)
    acc = None
    for j in range(n_split):
        cols = slice(j * fw, (j + 1) * fw)
        gate = jnp.dot(hb, wg_ref[:, cols], preferred_element_type=F32)
        up = jnp.dot(hb, wu_ref[:, cols], preferred_element_type=F32)
        gbuf[HALO:HALO + tm, cols] = gate
        conv = _causal_conv(gbuf.at[:, cols], cw_ref[:, cols], tm, FFN_CONV) + cb_ref[:, cols]
        gbuf[0:HALO, cols] = gbuf[tm:tm + HALO, cols]
        act = (_gelu_tanh(conv) * up).astype(BF16)
        part = jnp.dot(act, wd_ref[cols, :], preferred_element_type=F32)
        acc = part if acc is None else acc + part
    pre = alpha * x1 + (1.0 + gt_ref[0]) * acc
    o_ref[...] = _layer_norm(pre, lg_ref[...], lb_ref[...])


def _tail(x2, ya, yb, gates, gt1, wo, l1g, l1b, sc, sh, gt, wg, wu, cw, cb, wd, lg, lb, bsz, seq, tm, alpha, n_split):
    m, d = x2.shape
    d_ff = wg.shape[1]
    nst = seq // tm
    row = lambda b, s: (b * nst + s, 0)
    mod = lambda b, s: (b, 0, 0)
    return pl.pallas_call(
        functools.partial(_tail_kernel, alpha, n_split),
        out_shape=jax.ShapeDtypeStruct((m, d), F32),
        grid=(bsz, nst),
        in_specs=[pl.BlockSpec((tm, d), row), pl.BlockSpec((tm, d), row), pl.BlockSpec((tm, d), row),
                  pl.BlockSpec((tm, d), lambda b, s: (b * nst + s, 0)),
                  pl.BlockSpec((tm, d), lambda b, s: (b * nst + s, 1)),
                  pl.BlockSpec((1, 1, d), mod), _resident((d, d)), _resident((1, d)), _resident((1, d)),
                  pl.BlockSpec((1, 1, d), mod), pl.BlockSpec((1, 1, d), mod), pl.BlockSpec((1, 1, d), mod),
                  _resident((d, d_ff)), _resident((d, d_ff)), _resident((FFN_CONV, d_ff)), _resident((1, d_ff)),
                  _resident((d_ff, d)), _resident((1, d)), _resident((1, d))],
        out_specs=pl.BlockSpec((tm, d), row),
        scratch_shapes=[pltpu.VMEM((tm + HALO, d_ff), F32)],
        compiler_params=_cparams(("arbitrary", "arbitrary")),
        name="tail",
    )(x2, ya, yb, gates, gates, gt1, wo, l1g, l1b, sc, sh, gt, wg, wu, cw, cb, wd, lg, lb)


def _block_diag(w):
    n, bi, bj = w.shape
    eye = jnp.eye(n, dtype=w.dtype)
    return (eye[:, None, :, None] * w[:, :, None, :]).reshape(n * bi, n * bj)


def _prep_rg_gates(w_a, w_x, k_starts):
    da, dx = _block_diag(w_a), _block_diag(w_x)
    tiles = []
    for j, k0 in enumerate(k_starts):
        cols = slice(j * RG_TILE_N, (j + 1) * RG_TILE_N)
        tiles.append(jnp.concatenate([da[k0:k0 + RG_TILE_K, cols], dx[k0:k0 + RG_TILE_K, cols]], axis=1))
    return jnp.stack(tiles).astype(BF16)


def _group_major(q, k, v, z, n_groups):
    lead = q.shape[:-1]
    parts = [q.reshape(lead + (n_groups, -1)), k.reshape(lead + (n_groups, -1)),
             v.reshape(lead + (n_groups, -1)), z.reshape(lead + (n_groups, -1))]
    return jnp.concatenate(parts, axis=-1).reshape(lead + (-1,))


def kernel(x, c, w_ada, b_ada, w_in, rg_conv_w, rg_conv_b, rg_w_a, rg_b_a, rg_w_x, rg_b_x, rg_lambda, dn_conv_w, dn_a_log, dn_dt_bias, dn_norm_w, w_proj_a, w_proj_b, w_out, ln1_g, ln1_b, ffn_w_gate, ffn_w_up, ffn_conv_w, ffn_conv_b, ffn_w_down, ln2_g, ln2_b):
    bsz, seq, d = x.shape
    depth = w_ada.shape[0]
    m = bsz * seq
    d_rnn = rg_conv_w.shape[2]
    n_heads = dn_a_log.shape[1]
    n_groups = n_heads // 2
    dn_v = n_heads * DN_DV
    dn_qk = (dn_conv_w.shape[2] - dn_v) // 2
    assert dn_qk == n_groups * DN_DK
    alpha = (2 * depth) ** 0.25
    k_starts = _rg_windows(d_rnn)

    c_q = 2 * d_rnn
    c_k = c_q + dn_qk
    c_v = c_k + dn_qk
    c_z = c_v + dn_v
    c_ab = c_z + dn_v
    c_ga = c_ab + 2 * n_heads

    ts_a = min(512, seq)
    tm_q = min(512, seq)
    tt_gdn = min(256, seq)
    tm_tail = min(512, seq)

    x2 = x.reshape(m, d)
    for l in range(depth):
        ada = _adaln(c, w_ada[l], b_ada[l])
        sh1, sc1, gt1, sh2, sc2, gt2 = [t.reshape(bsz, 1, d) for t in jnp.split(ada, 6, axis=-1)]

        wi = w_in[l]
        w_a = jnp.concatenate([wi[:, :c_q], wi[:, c_ga:], wi[:, c_ab:c_ga],
                               jnp.zeros((d, AB_PAD - 2 * n_heads), wi.dtype)], axis=1).astype(BF16)
        wg = _prep_rg_gates(rg_w_a[l], rg_w_x[l], k_starts)
        pad_h = lambda v: jnp.zeros((1, LANES), F32).at[0, :n_heads].set(v)
        ya, gates, ab = _mixer_a(x2, sc1, sh1, w_a, rg_conv_w[l], rg_conv_b[l].reshape(1, d_rnn), wg,
                                 rg_b_a[l].reshape(1, d_rnn), rg_b_x[l].reshape(1, d_rnn),
                                 rg_lambda[l].reshape(1, d_rnn), w_proj_a[l].astype(BF16),
                                 pad_h(dn_a_log[l]), pad_h(dn_dt_bias[l]), bsz, seq, ts_a, k_starts, 2 * d, n_heads)

        w_g = _group_major(wi[:, c_q:c_k], wi[:, c_k:c_v], wi[:, c_v:c_z], wi[:, c_z:c_ab], n_groups).astype(BF16)
        cwl = dn_conv_w[l]
        cw_g = _group_major(cwl[:, :dn_qk], cwl[:, dn_qk:2 * dn_qk], cwl[:, 2 * dn_qk:],
                            jnp.zeros((DN_CONV, dn_v), cwl.dtype), n_groups)
        xg = _qkvz(x2, sc1, sh1, w_g, cw_g, bsz, seq, tm_q, n_groups)

        yb = _gdn(xg.reshape(n_groups, bsz, seq, GROUP_W), ab.reshape(bsz, seq, AB_PAD),
                  dn_norm_w[l].reshape(1, DN_DV), w_proj_b[l].astype(BF16), n_heads, tt_gdn).reshape(m, d)

        d_ff = ffn_w_gate.shape[2]
        x2 = _tail(x2, ya, yb, gates, gt1, w_out[l].astype(BF16), ln1_g[l].reshape(1, d), ln1_b[l].reshape(1, d),
                   sc2, sh2, gt2, ffn_w_gate[l].astype(BF16), ffn_w_up[l].astype(BF16), ffn_conv_w[l],
                   ffn_conv_b[l].reshape(1, d_ff), ffn_w_down[l].astype(BF16), ln2_g[l].reshape(1, d),
                   ln2_b[l].reshape(1, d), bsz, seq, tm_tail, alpha, 2)
    return x2.reshape(bsz, seq, d)
```

```python
import functools

import jax
import jax.numpy as jnp
from jax import lax
from jax.experimental import pallas as pl
from jax.experimental.pallas import tpu as pltpu

F32 = jnp.float32
BF16 = jnp.bfloat16

LANES = 128
SUBLANES = 8
VMEM_LIMIT_BYTES = 56 * 1024 * 1024

RG_BLOCKS = 16
RG_CONV = 4
RG_C = 8.0
DN_DK = 128
DN_DV = 128
DN_CONV = 4
CHUNK = 64
FFN_CONV = 3
LN_EPS = 1e-5
RMS_EPS = 1e-6
L2_EPS = 1e-6
HALO = SUBLANES
AB_PAD = LANES
GROUP_W = 2 * DN_DK + 4 * DN_DV
CONV_ROWS = 128


def _cparams(sem):
    return pltpu.CompilerParams(dimension_semantics=sem, vmem_limit_bytes=VMEM_LIMIT_BYTES)


def _resident(shape):
    return pl.BlockSpec(shape, lambda *_: (0,) * len(shape), pipeline_mode=pl.Buffered(1))


def _sigmoid(x):
    return 0.5 * jnp.tanh(0.5 * x) + 0.5


def _silu(x):
    h = 0.5 * x
    return h * jnp.tanh(h) + h


def _softplus(x):
    return jnp.maximum(x, 0.0) + jnp.log(1.0 + jnp.exp(-jnp.abs(x)))


def _gelu_tanh(x):
    h = 0.5 * x
    return h * jnp.tanh(x * (0.7978845608028654 + (0.7978845608028654 * 0.044715) * (x * x))) + h


def _layer_norm(x, g, b):
    mu = jnp.mean(x, axis=-1, keepdims=True)
    xc = x - mu
    var = jnp.mean(xc * xc, axis=-1, keepdims=True)
    return xc * lax.rsqrt(var + LN_EPS) * g + b


def _causal_conv(prev, x, w):
    width = w.shape[0]
    ext = jnp.concatenate([prev, x], axis=0)
    acc = w[width - 1:width, :] * x
    for d in range(1, width):
        acc = acc + w[width - 1 - d:width - d, :] * pltpu.roll(ext, d, axis=0)[HALO:]
    return acc


def _chunk_cumsum(x):
    row = lax.broadcasted_iota(jnp.int32, x.shape, 0) % CHUNK
    d = 1
    while d < CHUNK:
        x = x + jnp.where(row >= d, pltpu.roll(x, d, axis=0), 0.0)
        d *= 2
    return x


def _modulated(x_ref, sc_ref, sh_ref):
    return (x_ref[...] * (1.0 + sc_ref[0]) + sh_ref[0]).astype(BF16)


def _adaln_kernel(c_ref, w_ref, b_ref, o_ref):
    c = c_ref[...]
    o_ref[...] = jnp.dot(_silu(c), w_ref[...], preferred_element_type=F32,
                         precision=lax.Precision.HIGHEST) + b_ref[...]


def _adaln(c, w_ada, b_ada):
    bsz, d = c.shape
    n = w_ada.shape[1]
    rows = -(-bsz // SUBLANES) * SUBLANES
    c_pad = jnp.zeros((rows, d), F32).at[:bsz].set(c)
    tn = 1536
    out = pl.pallas_call(
        _adaln_kernel,
        out_shape=jax.ShapeDtypeStruct((rows, n), F32),
        grid=(n // tn,),
        in_specs=[pl.BlockSpec((rows, d), lambda j: (0, 0)),
                  pl.BlockSpec((d, tn), lambda j: (0, j)),
                  pl.BlockSpec((1, tn), lambda j: (0, j))],
        out_specs=pl.BlockSpec((rows, tn), lambda j: (0, j)),
        compiler_params=_cparams(("arbitrary",)),
        name="adaln",
    )(c_pad, w_ada, b_ada.reshape(1, n))
    return out[:bsz]


RG_TILE_N = 256
RG_TILE_K = 512


def _rg_windows(d_rnn):
    bw = d_rnn // RG_BLOCKS
    starts = []
    for j in range(d_rnn // RG_TILE_N):
        lo = (j * RG_TILE_N) // bw * bw
        hi = -(-((j + 1) * RG_TILE_N) // bw) * bw
        k0 = min(lo // LANES * LANES, d_rnn - RG_TILE_K)
        assert k0 <= lo and hi <= k0 + RG_TILE_K
        starts.append(k0)
    return tuple(starts)


def _mixer_a_kernel(k_starts, n_heads, x_ref, sc_ref, sh_ref, w_ref, cw_ref, cb_ref, wg_ref, ba_ref, bx_ref, lam_ref,
                    wp_ref, alog_ref, dtb_ref, ya_ref, gate_ref, ab_ref, xhalo, a_scr, u_scr, g_scr, h_scr):
    ts = x_ref.shape[0]
    d_rnn = cw_ref.shape[1]
    n_gate = gate_ref.shape[1]
    s = pl.program_id(1)

    @pl.when(s == 0)
    def _():
        xhalo[...] = jnp.zeros_like(xhalo)
        h_scr[...] = jnp.zeros_like(h_scr)

    hb = _modulated(x_ref, sc_ref, sh_ref)
    c_gate = 2 * d_rnn
    c_ab = c_gate + n_gate
    xr = jnp.dot(hb, w_ref[:, 0:d_rnn], preferred_element_type=F32)
    g_scr[...] = _gelu_tanh(jnp.dot(hb, w_ref[:, d_rnn:c_gate], preferred_element_type=F32))
    gate_ref[...] = _sigmoid(jnp.dot(hb, w_ref[:, c_gate:c_ab], preferred_element_type=F32)).astype(gate_ref.dtype)
    ab = jnp.dot(hb, w_ref[:, c_ab:c_ab + AB_PAD], preferred_element_type=F32)
    log_decay = -jnp.exp(alog_ref[...]) * _softplus(ab + dtb_ref[...])
    ab_lane = lax.broadcasted_iota(jnp.int32, ab.shape, 1)
    ab_ref[...] = jnp.where(ab_lane < n_heads, _chunk_cumsum(log_decay), _sigmoid(ab))

    xc = _causal_conv(xhalo[...], xr, cw_ref[...]) + cb_ref[...]
    xhalo[...] = xr[ts - HALO:ts, :]

    xcb = xc.astype(BF16)
    pre_r, pre_i = [], []
    for j, k0 in enumerate(k_starts):
        r = jnp.dot(xcb[:, k0:k0 + RG_TILE_K], wg_ref[j], preferred_element_type=F32)
        pre_r.append(r[:, :RG_TILE_N])
        pre_i.append(r[:, RG_TILE_N:])
    gate_r = _sigmoid(jnp.concatenate(pre_r, axis=1) + ba_ref[...])
    gate_i = _sigmoid(jnp.concatenate(pre_i, axis=1) + bx_ref[...])
    log_a = (-RG_C) * gate_r * _softplus(-lam_ref[...])
    a_scr[...] = jnp.exp(log_a)
    u_scr[...] = jnp.sqrt(1.0 - jnp.exp(2.0 * log_a)) * gate_i * xc

    row = lax.broadcasted_iota(jnp.int32, (SUBLANES, d_rnn), 0)

    def group(g, h_prev):
        r0 = pl.multiple_of(g * SUBLANES, SUBLANES)
        ag = a_scr[pl.ds(r0, SUBLANES), :]
        ug = u_scr[pl.ds(r0, SUBLANES), :]
        for d in (1, 2, 4):
            keep = row >= d
            a_sh = jnp.where(keep, pltpu.roll(ag, d, axis=0), 1.0)
            u_sh = jnp.where(keep, pltpu.roll(ug, d, axis=0), 0.0)
            ug = ug + ag * u_sh
            ag = ag * a_sh
        hg = ug + ag * h_prev
        u_scr[pl.ds(r0, SUBLANES), :] = hg
        return jnp.broadcast_to(hg[SUBLANES - 1:SUBLANES, :], (SUBLANES, d_rnn))

    h_scr[...] = lax.fori_loop(0, ts // SUBLANES, group, h_scr[...], unroll=ts // SUBLANES)

    rec = u_scr[...] * g_scr[...]
    ya_ref[...] = jnp.dot(rec.astype(BF16), wp_ref[...], preferred_element_type=F32).astype(ya_ref.dtype)


def _mixer_a(x2, sc, sh, w, cw, cb, wg, ba, bx, lam, wp, alog, dtb, bsz, seq, ts, k_starts, n_gate, n_heads):
    m, d = x2.shape
    d_rnn = cw.shape[1]
    nst = seq // ts
    row = lambda b, s: (b * nst + s, 0)
    mod = lambda b, s: (b, 0, 0)
    return pl.pallas_call(
        functools.partial(_mixer_a_kernel, k_starts, n_heads),
        out_shape=(jax.ShapeDtypeStruct((m, d), BF16),
                   jax.ShapeDtypeStruct((m, n_gate), BF16),
                   jax.ShapeDtypeStruct((m, AB_PAD), F32)),
        grid=(bsz, nst),
        in_specs=[pl.BlockSpec((ts, d), row), pl.BlockSpec((1, 1, d), mod), pl.BlockSpec((1, 1, d), mod),
                  _resident(w.shape), _resident(cw.shape), _resident((1, d_rnn)), _resident(wg.shape),
                  _resident((1, d_rnn)), _resident((1, d_rnn)), _resident((1, d_rnn)), _resident(wp.shape),
                  _resident((1, LANES)), _resident((1, LANES))],
        out_specs=(pl.BlockSpec((ts, d), row), pl.BlockSpec((ts, n_gate), row), pl.BlockSpec((ts, AB_PAD), row)),
        scratch_shapes=[pltpu.VMEM((HALO, d_rnn), F32),
                        pltpu.VMEM((ts, d_rnn), F32),
                        pltpu.VMEM((ts, d_rnn), F32),
                        pltpu.VMEM((ts, d_rnn), F32),
                        pltpu.VMEM((SUBLANES, d_rnn), F32)],
        compiler_params=_cparams(("arbitrary", "arbitrary")),
        name="mixer_a",
    )(x2, sc, sh, w, cw, cb, wg, ba, bx, lam, wp, alog, dtb)


def _qkvz_kernel(x_ref, sc_ref, sh_ref, w_ref, cw_ref, o_ref, halo):
    tm = x_ref.shape[0]
    n_groups = o_ref.shape[0]
    conv_w = 2 * DN_DK + 2 * DN_DV
    s = pl.program_id(1)

    @pl.when(s == 0)
    def _():
        halo[...] = jnp.zeros_like(halo)

    hb = _modulated(x_ref, sc_ref, sh_ref)
    for g in range(n_groups):
        c0 = g * GROUP_W
        acc = jnp.dot(hb, w_ref[:, c0:c0 + GROUP_W], preferred_element_type=F32)
        o_ref[g, :, conv_w:] = acc[:, conv_w:].astype(o_ref.dtype)
        cw = cw_ref[:, c0:c0 + conv_w]
        ext = jnp.concatenate([halo[g], acc[:, 0:conv_w]], axis=0)
        halo[g] = acc[tm - HALO:tm, 0:conv_w]
        for r0 in range(0, tm, CONV_ROWS):
            blk = _causal_conv(ext[r0:r0 + HALO], ext[r0 + HALO:r0 + HALO + CONV_ROWS], cw)
            o_ref[g, r0:r0 + CONV_ROWS, 0:conv_w] = blk.astype(o_ref.dtype)


def _qkvz(x2, sc, sh, w, cw, bsz, seq, tm, n_groups):
    m, d = x2.shape
    nst = seq // tm
    mod = lambda b, s: (b, 0, 0)
    conv_w = 2 * DN_DK + 2 * DN_DV
    return pl.pallas_call(
        _qkvz_kernel,
        out_shape=jax.ShapeDtypeStruct((n_groups, m, GROUP_W), BF16),
        grid=(bsz, nst),
        in_specs=[pl.BlockSpec((tm, d), lambda b, s: (b * nst + s, 0)),
                  pl.BlockSpec((1, 1, d), mod), pl.BlockSpec((1, 1, d), mod),
                  _resident(w.shape), _resident(cw.shape)],
        out_specs=pl.BlockSpec((n_groups, tm, GROUP_W), lambda b, s: (0, b * nst + s, 0)),
        scratch_shapes=[pltpu.VMEM((n_groups, HALO, conv_w), F32)],
        compiler_params=_cparams(("arbitrary", "arbitrary")),
        name="qkvz",
    )(x2, sc, sh, w, cw)


def _pair_blockdiag(m2, left):
    return jnp.concatenate([jnp.where(left, m2, 0.0), jnp.where(left, 0.0, m2)], axis=0)


def _gdn_kernel(n_groups, x_ref, ab_ref, nw_ref, wb_ref,
                o_ref,
                s_scr, u_scr, w_scr, qd_scr, qk_scr, kdt_scr, gl_scr, z_scr, dn_scr):
    bsz, tt, _ = x_ref.shape
    n_chunks = tt // CHUNK
    t = pl.program_id(0)
    hg = t % n_groups
    hg_r = (t + n_groups - 1) % n_groups
    slot, slot_r = t % 2, (t + 1) % 2
    c_k, c_v, c_z = DN_DK, 2 * DN_DK, 2 * DN_DK + 2 * DN_DV

    @pl.when(t == 0)
    def _():
        for ref in (u_scr, w_scr, qd_scr, qk_scr, kdt_scr, gl_scr, z_scr):
            ref[1] = jnp.zeros(ref.shape[1:], ref.dtype)

    @pl.when(t <= n_groups)
    def _():
        s_scr[2 * hg_r] = jnp.zeros(s_scr.shape[1:], F32)
        s_scr[2 * hg_r + 1] = jnp.zeros(s_scr.shape[1:], F32)

    lane = lax.broadcasted_iota(jnp.int32, (CHUNK, LANES), 1)
    row = lax.broadcasted_iota(jnp.int32, (CHUNK, LANES), 0)
    left = lane < CHUNK
    jj = lane % CHUNK
    causal = row >= jj
    strict = row > jj
    eye2 = jnp.where(row == jj, 1.0, 0.0)

    def lane_col(arr, col):
        r = pltpu.roll(arr, (LANES - col) % LANES, axis=1)
        return jnp.broadcast_to(r[:, 0:1], arr.shape)

    n_heads = s_scr.shape[0]
    n_steps = CHUNK.bit_length() - 2
    nw = nw_ref[...]
    zeros_v = jnp.zeros((CHUNK, DN_DV), F32)

    def load_rows(b):
        act = _silu(x_ref[b, :, 0:c_z].astype(F32))
        q, k = act[:, 0:c_k], act[:, c_k:c_v]
        qn = q * (lax.rsqrt(jnp.sum(q * q, axis=-1, keepdims=True) + L2_EPS) * (DN_DK ** -0.5))
        kn = k * lax.rsqrt(jnp.sum(k * k, axis=-1, keepdims=True) + L2_EPS)
        ab = ab_ref[b]
        gb = [lane_col(ab, 2 * hg + j) for j in range(2)]
        bb = [lane_col(ab, n_heads + 2 * hg + j) for j in range(2)]
        return dict(qn=qn, kn=kn, v=act[:, c_v:c_z], gb=gb, bb=bb)

    def start_chain(ch):
        rows, c = ch["rows"], ch["c"]
        rs = slice(c * CHUNK, (c + 1) * CHUNK)
        ch.update(qn=rows["qn"][rs], kn=rows["kn"][rs], v=rows["v"][rs],
                  g=(rows["gb"][0][rs], rows["gb"][1][rs]), beta=(rows["bb"][0][rs], rows["bb"][1][rs]))
        kn_b = ch["kn"].astype(BF16)
        qk_lhs = jnp.concatenate([ch["qn"].astype(BF16), kn_b], axis=0)
        kk_rhs = jnp.concatenate([kn_b, kn_b], axis=0)
        kq = lax.dot_general(qk_lhs, kk_rhs, (((1,), (1,)), ((), ())), preferred_element_type=F32)
        g0, g1 = ch["g"]
        g_pair = jnp.where(left, g0, g1)
        b_pair = jnp.where(left, ch["beta"][0], ch["beta"][1])
        g_rows = jnp.transpose(jnp.concatenate([g0, g1], axis=0))[0:CHUNK]
        decay = jnp.where(causal, jnp.exp(jnp.where(causal, g_pair - g_rows, 0.0)), 0.0)
        ch["qk2"] = (kq[0:CHUNK] * decay).astype(BF16)
        m = -jnp.where(strict, b_pair * kq[CHUNK:2 * CHUNK] * decay, 0.0)
        ch["qsum"] = eye2 + m
        ch["m"] = jnp.dot(m.astype(BF16), _pair_blockdiag(m, left).astype(BF16), preferred_element_type=F32)

    def inverse_step(ch, it):
        bd = _pair_blockdiag(ch["m"], left).astype(BF16)
        if it < n_steps - 1:
            r = jnp.dot(jnp.concatenate([ch["qsum"], ch["m"]], axis=0).astype(BF16), bd, preferred_element_type=F32)
            ch["qsum"] = ch["qsum"] + r[0:CHUNK]
            ch["m"] = r[CHUNK:2 * CHUNK]
        else:
            ch["qsum"] = ch["qsum"] + jnp.dot(ch["qsum"].astype(BF16), bd, preferred_element_type=F32)

    def finish_chain(ch):
        b, c, (g0, g1), (b0, b1), kn_c, qn_c = ch["b"], ch["c"], ch["g"], ch["beta"], ch["kn"], ch["qn"]
        eg = (jnp.exp(g0), jnp.exp(g1))
        v0, v1 = ch["v"][:, 0:DN_DV], ch["v"][:, DN_DV:2 * DN_DV]
        rhs = jnp.concatenate([jnp.concatenate([b0 * v0, (b0 * eg[0]) * kn_c], axis=1),
                               jnp.concatenate([b1 * v1, (b1 * eg[1]) * kn_c], axis=1)], axis=0)
        uw = jnp.dot(_pair_blockdiag(ch["qsum"], left).astype(BF16), rhs.astype(BF16), preferred_element_type=F32)
        gl = (jnp.broadcast_to(g0[CHUNK - 1:CHUNK], g0.shape), jnp.broadcast_to(g1[CHUNK - 1:CHUNK], g1.shape))
        kd = jnp.concatenate([kn_c * jnp.exp(gl[0] - g0), kn_c * jnp.exp(gl[1] - g1)], axis=0)
        kdt_scr[slot, b, c] = jnp.transpose(kd).astype(BF16)
        qk_scr[slot, b, c] = ch["qk2"]
        for j in range(2):
            u_scr[slot, j, b, c] = uw[j * CHUNK:(j + 1) * CHUNK, 0:DN_DV]
            w_scr[slot, j, b, c] = uw[j * CHUNK:(j + 1) * CHUNK, DN_DV:2 * DN_DV].astype(BF16)
            qd_scr[slot, j, b, c] = (qn_c * eg[j]).astype(BF16)
            gl_scr[slot, j, b, c] = jnp.exp(gl[j][0:SUBLANES])
        rs = slice(c * CHUNK, (c + 1) * CHUNK)
        z_scr[slot, b, c] = _silu(x_ref[b, rs, c_z:c_z + 2 * DN_DV].astype(F32)).astype(BF16)

    rec = {}

    def rec_a(c, b):
        for j in range(2):
            rec["st", b, j] = s_scr[2 * hg_r + j, b]
            lhs = jnp.concatenate([w_scr[slot_r, j, b, c], qd_scr[slot_r, j, b, c]], axis=0)
            rec["r1", b, j] = jnp.dot(lhs, rec["st", b, j].astype(BF16), preferred_element_type=F32)

    def rec_b(c, b):
        vnew = [u_scr[slot_r, j, b, c] - rec["r1", b, j][0:CHUNK] for j in range(2)]
        bdv = jnp.concatenate([jnp.concatenate([vnew[0], zeros_v], axis=1),
                               jnp.concatenate([zeros_v, vnew[1]], axis=1)], axis=0).astype(BF16)
        lhs2 = jnp.concatenate([qk_scr[slot_r, b, c], kdt_scr[slot_r, b, c]], axis=0)
        rec["r2", b] = jnp.dot(lhs2, bdv, preferred_element_type=F32)

    def rec_c(c, b):
        outs = []
        for j in range(2):
            cols = slice(j * DN_DV, (j + 1) * DN_DV)
            r1, r2 = rec.pop(("r1", b, j)), rec["r2", b]
            o = r1[CHUNK:2 * CHUNK] + r2[0:CHUNK, cols]
            s_scr[2 * hg_r + j, b] = gl_scr[slot_r, j, b, c][0:1, :] * rec.pop(("st", b, j)) + r2[CHUNK:CHUNK + DN_DK, cols]
            o = o * lax.rsqrt(jnp.mean(o * o, axis=-1, keepdims=True) + RMS_EPS) * nw
            outs.append(o)
        dn_scr[hg_r, b, c * CHUNK:(c + 1) * CHUNK, :] = (jnp.concatenate(outs, axis=1) * z_scr[slot_r, b, c]).astype(BF16)

    def merged(*streams):
        total = max(len(st) for st in streams)
        order = sorted((i * total // len(st), si, i) for si, st in enumerate(streams) for i in range(len(st)))
        for _, si, i in order:
            streams[si][i]()

    def inverse_stream(chs):
        return [functools.partial(inverse_step, ch, it) for it in range(n_steps) for ch in chs]

    def rec_stream(cs):
        return [functools.partial(f, c, b) for c in cs for f in (rec_a, rec_b, rec_c) for b in range(bsz)]

    rows = [{} for _ in range(bsz)]
    chains = [dict(rows=rows[b], b=b, c=c) for c in range(n_chunks) for b in range(bsz)]
    prep = ([functools.partial(lambda b: rows[b].update(load_rows(b)), b) for b in range(bsz)]
            + [functools.partial(start_chain, ch) for ch in chains]
            + inverse_stream(chains)
            + [functools.partial(finish_chain, ch) for ch in chains])
    merged(prep, rec_stream(range(n_chunks)))

    @pl.when((hg_r == n_groups - 1) & (t > 0))
    def _():
        for b in range(bsz):
            dn = jnp.concatenate([dn_scr[g, b] for g in range(dn_scr.shape[0])], axis=1)
            o_ref[b] = jnp.dot(dn, wb_ref[...], preferred_element_type=F32).astype(o_ref.dtype)


def _gdn(xg, ab3, nw, wb, n_heads, tt):
    n_groups, bsz, seq, _ = xg.shape
    d = wb.shape[1]
    nst = seq // tt
    nch = tt // CHUNK
    n_steps = nst * n_groups
    last = n_steps - 1
    in_specs = [
        pl.BlockSpec((None, bsz, tt, GROUP_W),
                     lambda t: (jnp.minimum(t, last) % n_groups, 0, jnp.minimum(t, last) // n_groups, 0)),
        pl.BlockSpec((bsz, tt, AB_PAD), lambda t: (0, jnp.minimum(t, last) // n_groups, 0)),
        _resident((1, DN_DV)), _resident(wb.shape),
    ]
    scratch = [
        pltpu.VMEM((n_heads, bsz, DN_DK, DN_DV), F32),
        pltpu.VMEM((2, 2, bsz, nch, CHUNK, DN_DV), F32),
        pltpu.VMEM((2, 2, bsz, nch, CHUNK, DN_DK), BF16),
        pltpu.VMEM((2, 2, bsz, nch, CHUNK, DN_DK), BF16),
        pltpu.VMEM((2, bsz, nch, CHUNK, 2 * CHUNK), BF16),
        pltpu.VMEM((2, bsz, nch, DN_DK, 2 * CHUNK), BF16),
        pltpu.VMEM((2, 2, bsz, nch, SUBLANES, DN_DV), F32),
        pltpu.VMEM((2, bsz, nch, CHUNK, 2 * DN_DV), BF16),
        pltpu.VMEM((n_groups, bsz, tt, 2 * DN_DV), BF16),
    ]
    return pl.pallas_call(
        functools.partial(_gdn_kernel, n_groups),
        out_shape=jax.ShapeDtypeStruct((bsz, seq, d), BF16),
        grid=(n_steps + 1,),
        in_specs=in_specs,
        out_specs=pl.BlockSpec((bsz, tt, d), lambda t: (0, jnp.maximum(t - 1, 0) // n_groups, 0)),
        scratch_shapes=scratch,
        compiler_params=_cparams(("arbitrary",)),
        name="gdn",
    )(xg, ab3, nw, wb)


def _tail_kernel(alpha, n_split, x_ref, ya_ref, yb_ref, ga_ref, gb_ref, gt1_ref, wo_ref,
                 l1g_ref, l1b_ref, sc_ref, sh_ref, gt_ref, wg_ref, wu_ref, cw_ref, cb_ref, wd_ref, lg_ref, lb_ref,
                 o_ref, ghalo):
    tm, d = x_ref.shape
    d_ff = wg_ref.shape[1]
    fw = d_ff // n_split
    s = pl.program_id(1)

    @pl.when(s == 0)
    def _():
        ghalo[...] = jnp.zeros_like(ghalo)

    merged = (ga_ref[...].astype(F32) * ya_ref[...].astype(F32)
              + gb_ref[...].astype(F32) * yb_ref[...].astype(F32))
    mix = jnp.dot(merged.astype(BF16), wo_ref[...], preferred_element_type=F32)
    x1 = _layer_norm(alpha * x_ref[...] + (1.0 + gt1_ref[0]) * mix, l1g_ref[...], l1b_ref[...])

    hb = (x1 * (1.0 + sc_ref[0]) + sh_ref[0]).astype(BF16)
    acc = None
    for j in range(n_split):
        cols = slice(j * fw, (j + 1) * fw)
        gate = jnp.dot(hb, wg_ref[:, cols], preferred_element_type=F32)
        up = jnp.dot(hb, wu_ref[:, cols], preferred_element_type=F32)
        conv = _causal_conv(ghalo[:, cols], gate, cw_ref[:, cols]) + cb_ref[:, cols]
        ghalo[:, cols] = gate[tm - HALO:tm, :]
        act = (_gelu_tanh(conv) * up).astype(BF16)
        part = jnp.dot(act, wd_ref[cols, :], preferred_element_type=F32)
        acc = part if acc is None else acc + part
    pre = alpha * x1 + (1.0 + gt_ref[0]) * acc
    o_ref[...] = _layer_norm(pre, lg_ref[...], lb_ref[...])


def _tail(x2, ya, yb, gates, gt1, wo, l1g, l1b, sc, sh, gt, wg, wu, cw, cb, wd, lg, lb, bsz, seq, tm, alpha, n_split):
    m, d = x2.shape
    d_ff = wg.shape[1]
    nst = seq // tm
    row = lambda b, s: (b * nst + s, 0)
    mod = lambda b, s: (b, 0, 0)
    return pl.pallas_call(
        functools.partial(_tail_kernel, alpha, n_split),
        out_shape=jax.ShapeDtypeStruct((m, d), F32),
        grid=(bsz, nst),
        in_specs=[pl.BlockSpec((tm, d), row), pl.BlockSpec((tm, d), row), pl.BlockSpec((tm, d), row),
                  pl.BlockSpec((tm, d), lambda b, s: (b * nst + s, 0)),
                  pl.BlockSpec((tm, d), lambda b, s: (b * nst + s, 1)),
                  pl.BlockSpec((1, 1, d), mod), _resident((d, d)), _resident((1, d)), _resident((1, d)),
                  pl.BlockSpec((1, 1, d), mod), pl.BlockSpec((1, 1, d), mod), pl.BlockSpec((1, 1, d), mod),
                  _resident((d, d_ff)), _resident((d, d_ff)), _resident((FFN_CONV, d_ff)), _resident((1, d_ff)),
                  _resident((d_ff, d)), _resident((1, d)), _resident((1, d))],
        out_specs=pl.BlockSpec((tm, d), row),
        scratch_shapes=[pltpu.VMEM((HALO, d_ff), F32)],
        compiler_params=_cparams(("arbitrary", "arbitrary")),
        name="tail",
    )(x2, ya, yb, gates, gates, gt1, wo, l1g, l1b, sc, sh, gt, wg, wu, cw, cb, wd, lg, lb)


def _block_diag(w):
    n, bi, bj = w.shape
    eye = jnp.eye(n, dtype=w.dtype)
    return (eye[:, None, :, None] * w[:, :, None, :]).reshape(n * bi, n * bj)


def _prep_rg_gates(w_a, w_x, k_starts):
    da, dx = _block_diag(w_a), _block_diag(w_x)
    tiles = []
    for j, k0 in enumerate(k_starts):
        cols = slice(j * RG_TILE_N, (j + 1) * RG_TILE_N)
        tiles.append(jnp.concatenate([da[k0:k0 + RG_TILE_K, cols], dx[k0:k0 + RG_TILE_K, cols]], axis=1))
    return jnp.stack(tiles).astype(BF16)


def _group_major(q, k, v, z, n_groups):
    lead = q.shape[:-1]
    parts = [q.reshape(lead + (n_groups, -1)), k.reshape(lead + (n_groups, -1)),
             v.reshape(lead + (n_groups, -1)), z.reshape(lead + (n_groups, -1))]
    return jnp.concatenate(parts, axis=-1).reshape(lead + (-1,))


def kernel(x, c, w_ada, b_ada, w_in, rg_conv_w, rg_conv_b, rg_w_a, rg_b_a, rg_w_x, rg_b_x, rg_lambda, dn_conv_w, dn_a_log, dn_dt_bias, dn_norm_w, w_proj_a, w_proj_b, w_out, ln1_g, ln1_b, ffn_w_gate, ffn_w_up, ffn_conv_w, ffn_conv_b, ffn_w_down, ln2_g, ln2_b):
    bsz, seq, d = x.shape
    depth = w_ada.shape[0]
    m = bsz * seq
    d_rnn = rg_conv_w.shape[2]
    n_heads = dn_a_log.shape[1]
    n_groups = n_heads // 2
    dn_v = n_heads * DN_DV
    dn_qk = (dn_conv_w.shape[2] - dn_v) // 2
    assert dn_qk == n_groups * DN_DK
    alpha = (2 * depth) ** 0.25
    k_starts = _rg_windows(d_rnn)

    c_q = 2 * d_rnn
    c_k = c_q + dn_qk
    c_v = c_k + dn_qk
    c_z = c_v + dn_v
    c_ab = c_z + dn_v
    c_ga = c_ab + 2 * n_heads

    ts_a = min(512, seq)
    tm_q = min(512, seq)
    tt_gdn = min(256, seq)
    tm_tail = min(512, seq)

    x2 = x.reshape(m, d)
    for l in range(depth):
        ada = _adaln(c, w_ada[l], b_ada[l])
        sh1, sc1, gt1, sh2, sc2, gt2 = [t.reshape(bsz, 1, d) for t in jnp.split(ada, 6, axis=-1)]

        wi = w_in[l].astype(BF16)
        w_a = jnp.concatenate([wi[:, :c_q], wi[:, c_ga:], wi[:, c_ab:c_ga],
                               jnp.zeros((d, AB_PAD - 2 * n_heads), wi.dtype)], axis=1)
        wg = _prep_rg_gates(rg_w_a[l], rg_w_x[l], k_starts)
        pad_h = lambda v: jnp.zeros((1, LANES), F32).at[0, :n_heads].set(v)
        ya, gates, ab = _mixer_a(x2, sc1, sh1, w_a, rg_conv_w[l], rg_conv_b[l].reshape(1, d_rnn), wg,
                                 rg_b_a[l].reshape(1, d_rnn), rg_b_x[l].reshape(1, d_rnn),
                                 rg_lambda[l].reshape(1, d_rnn), w_proj_a[l].astype(BF16),
                                 pad_h(dn_a_log[l]), pad_h(dn_dt_bias[l]), bsz, seq, ts_a, k_starts, 2 * d, n_heads)

        w_g = _group_major(wi[:, c_q:c_k], wi[:, c_k:c_v], wi[:, c_v:c_z], wi[:, c_z:c_ab], n_groups)
        cwl = dn_conv_w[l]
        cw_g = _group_major(cwl[:, :dn_qk], cwl[:, dn_qk:2 * dn_qk], cwl[:, 2 * dn_qk:],
                            jnp.zeros((DN_CONV, dn_v), cwl.dtype), n_groups)
        xg = _qkvz(x2, sc1, sh1, w_g, cw_g, bsz, seq, tm_q, n_groups)

        yb = _gdn(xg.reshape(n_groups, bsz, seq, GROUP_W), ab.reshape(bsz, seq, AB_PAD),
                  dn_norm_w[l].reshape(1, DN_DV), w_proj_b[l].astype(BF16), n_heads, tt_gdn).reshape(m, d)

        d_ff = ffn_w_gate.shape[2]
        x2 = _tail(x2, ya, yb, gates, gt1, w_out[l].astype(BF16), ln1_g[l].reshape(1, d), ln1_b[l].reshape(1, d),
                   sc2, sh2, gt2, ffn_w_gate[l].astype(BF16), ffn_w_up[l].astype(BF16), ffn_conv_w[l],
                   ffn_conv_b[l].reshape(1, d_ff), ffn_w_down[l].astype(BF16), ln2_g[l].reshape(1, d),
                   ln2_b[l].reshape(1, d), bsz, seq, tm_tail, alpha, 1)
    return x2.reshape(bsz, seq, d)
```
